```python
import jax, jax.numpy as jnp
from jax import lax
import numpy as np

D_MODEL = 4096
BATCH = 1
SEQ = 8192
DEPTH = 4

SSD_HEAD_DIM = 64
D_SSD = D_MODEL // 2
SSD_HEADS = D_SSD // SSD_HEAD_DIM
SSD_GROUPS = 4
SSD_STATE = 128
SSD_CONV = 5
SSD_CHUNK = 128
XBC_DIM = D_SSD + 2 * SSD_GROUPS * SSD_STATE
D_SC = D_MODEL // 4
SC_CONV = 3
D_FF = ((8 * D_MODEL + 3 * 256 - 1) // (3 * 256)) * 256
IN_COLS = D_SSD + XBC_DIM + 2 * SSD_HEADS + 3 * D_SC + 2 * D_MODEL
EPS = 1e-6

kernel_name = "hybrid_ssd_shortconv_gated_encoder"


def rms_norm(x, g):
    xf = x.astype(jnp.float32)
    y = xf * lax.rsqrt(jnp.mean(xf * xf, axis=-1, keepdims=True) + EPS)
    return (y * g.astype(jnp.float32)).astype(x.dtype)


def depthwise_conv_centred(u, w):
    k = w.shape[0]
    pad = k // 2
    length = u.shape[1]
    up = jnp.pad(u, ((0, 0), (pad, pad), (0, 0)))
    out = up[:, 0:length, :] * w[0]
    for t in range(1, k):
        out = out + up[:, t:t + length, :] * w[t]
    return out


def ssd_scan(x, dt, a_neg, bmat, cmat):
    b, length, h, p = x.shape
    g, n = bmat.shape[2], bmat.shape[3]
    r = h // g
    c = length // SSD_CHUNK
    xd = (x * dt[..., None]).reshape(b, c, SSD_CHUNK, g, r, p)
    a = (dt * a_neg).reshape(b, c, SSD_CHUNK, g, r).transpose(0, 3, 4, 1, 2)
    bc = bmat.reshape(b, c, SSD_CHUNK, g, n)
    cc = cmat.reshape(b, c, SSD_CHUNK, g, n)
    a_cum = jnp.cumsum(a, axis=-1)
    idx = jnp.arange(SSD_CHUNK)
    mask = idx[:, None] >= idx[None, :]
    diff = a_cum[..., :, None] - a_cum[..., None, :]
    lmat = jnp.where(mask, jnp.exp(jnp.where(mask, diff, 0.0)), 0.0)
    cb = jnp.einsum('bclgn,bcsgn->bgcls', cc, bc)
    y_diag = jnp.einsum('bgrcls,bcsgrp->bclgrp', cb[:, :, None] * lmat, xd)
    decay_to_end = jnp.exp(a_cum[..., -1:] - a_cum)
    chunk_states = jnp.einsum('bclgn,bgrcl,bclgrp->cbgrpn', bc, decay_to_end, xd)
    chunk_decay = jnp.exp(a_cum[..., -1]).transpose(3, 0, 1, 2)

    def step(state, inp):
        s_c, d_c = inp
        return state * d_c[..., None, None] + s_c, state

    init = jnp.zeros(chunk_states.shape[1:], jnp.float32)
    _, prev_states = lax.scan(step, init, (chunk_states, chunk_decay))
    y_off = jnp.einsum('bclgn,cbgrpn,bgrcl->bclgrp', cc, prev_states, jnp.exp(a_cum))
    return (y_diag + y_off).reshape(b, length, h, p)


def ssd_branch(z, xbc, dtf_raw, dtb_raw, conv_w, conv_b, dt_bias_f, dt_bias_b,
               a_log_f, a_log_b, d_skip, norm_g):
    out_dtype = z.dtype
    b, length, _ = z.shape
    f32 = jnp.float32
    xbc = jax.nn.silu(depthwise_conv_centred(xbc.astype(f32), conv_w.astype(f32)) + conv_b.astype(f32))
    gn = SSD_GROUPS * SSD_STATE
    xs = xbc[..., :D_SSD].reshape(b, length, SSD_HEADS, SSD_HEAD_DIM)
    bm = xbc[..., D_SSD:D_SSD + gn].reshape(b, length, SSD_GROUPS, SSD_STATE)
    cm = xbc[..., D_SSD + gn:].reshape(b, length, SSD_GROUPS, SSD_STATE)
    dt_f = jax.nn.softplus(dtf_raw.astype(f32) + dt_bias_f.astype(f32))
    dt_b = jax.nn.softplus(dtb_raw.astype(f32) + dt_bias_b.astype(f32))
    a_f = -jnp.exp(a_log_f.astype(f32))
    a_b = -jnp.exp(a_log_b.astype(f32))
    flip = lambda t: jnp.flip(t, axis=1)
    y_fwd = ssd_scan(xs, dt_f, a_f, bm, cm)
    y_bwd = flip(ssd_scan(flip(xs), flip(dt_b), a_b, flip(bm), flip(cm)))
    y = y_fwd + y_bwd + d_skip.astype(f32)[:, None] * xs
    y = y.reshape(b, length, D_SSD) * jax.nn.silu(z.astype(f32))
    yg = y.reshape(b, length, SSD_GROUPS, D_SSD // SSD_GROUPS)
    yg = yg * lax.rsqrt(jnp.mean(yg * yg, axis=-1, keepdims=True) + EPS)
    y = yg.reshape(b, length, D_SSD) * norm_g.astype(f32)
    return y.astype(out_dtype)


def short_conv_branch(u, conv_w):
    bg = u[..., :D_SC]
    cg = u[..., D_SC:2 * D_SC]
    hh = u[..., 2 * D_SC:]
    return bg * depthwise_conv_centred(cg * hh, conv_w)


def setup_inputs(seed: int = 0) -> dict:
    key = jax.random.key(seed)
    ks = jax.random.split(key, 24)
    f32 = jnp.float32
    nrm = lambda k, shape, scale: jax.random.normal(k, shape, f32) * scale
    dt0 = jnp.exp(jax.random.uniform(ks[9], (DEPTH, SSD_HEADS), f32) * (np.log(0.1) - np.log(0.001)) + np.log(0.001))
    dt1 = jnp.exp(jax.random.uniform(ks[10], (DEPTH, SSD_HEADS), f32) * (np.log(0.1) - np.log(0.001)) + np.log(0.001))
    return {
        "x": jax.random.normal(ks[0], (BATCH, SEQ, D_MODEL), f32),
        "mix_norm_g": 1.0 + nrm(ks[1], (DEPTH, D_MODEL), 0.01),
        "w_in": nrm(ks[2], (DEPTH, D_MODEL, IN_COLS), D_MODEL ** -0.5),
        "b_gate": nrm(ks[3], (DEPTH, 2 * D_MODEL), 0.01),
        "ssd_conv_w": nrm(ks[4], (DEPTH, SSD_CONV, XBC_DIM), SSD_CONV ** -0.5),
        "ssd_conv_b": nrm(ks[5], (DEPTH, XBC_DIM), 0.01),
        "dt_bias_f": dt0 + jnp.log(-jnp.expm1(-dt0)),
        "dt_bias_b": dt1 + jnp.log(-jnp.expm1(-dt1)),
        "a_log_f": jnp.log(jax.random.uniform(ks[6], (DEPTH, SSD_HEADS), f32, 1.0, 16.0)),
        "a_log_b": jnp.log(jax.random.uniform(ks[7], (DEPTH, SSD_HEADS), f32, 1.0, 16.0)),
        "d_skip": 1.0 + nrm(ks[8], (DEPTH, SSD_HEADS), 0.01),
        "ssd_norm_g": 1.0 + nrm(ks[11], (DEPTH, D_SSD), 0.01),
        "w_ssd_out": nrm(ks[12], (DEPTH, D_SSD, D_MODEL), D_SSD ** -0.5),
        "sc_conv_w": nrm(ks[13], (DEPTH, SC_CONV, D_SC), SC_CONV ** -0.5),
        "w_sc_out": nrm(ks[14], (DEPTH, D_SC, D_MODEL), D_SC ** -0.5),
        "w_o": nrm(ks[15], (DEPTH, D_MODEL, D_MODEL), D_MODEL ** -0.5),
        "ffn_norm_g": 1.0 + nrm(ks[16], (DEPTH, D_MODEL), 0.01),
        "w_ffn_gate": nrm(ks[17], (DEPTH, D_MODEL, D_FF), D_MODEL ** -0.5),
        "w_ffn_up": nrm(ks[18], (DEPTH, D_MODEL, D_FF), D_MODEL ** -0.5),
        "w_ffn_down": nrm(ks[19], (DEPTH, D_FF, D_MODEL), D_FF ** -0.5),
        "final_norm_g": 1.0 + nrm(ks[20], (D_MODEL,), 0.01),
    }


def reference(x, mix_norm_g, w_in, b_gate, ssd_conv_w, ssd_conv_b, dt_bias_f, dt_bias_b,
              a_log_f, a_log_b, d_skip, ssd_norm_g, w_ssd_out, sc_conv_w, w_sc_out, w_o,
              ffn_norm_g, w_ffn_gate, w_ffn_up, w_ffn_down, final_norm_g):
    o1 = D_SSD
    o2 = o1 + XBC_DIM
    o3 = o2 + SSD_HEADS
    o4 = o3 + SSD_HEADS
    o5 = o4 + 3 * D_SC
    for l in range(DEPTH):
        n = rms_norm(x, mix_norm_g[l])
        p = jnp.einsum('bsd,dc->bsc', n, w_in[l])
        z, xbc, dtf, dtb, sc_in, g_raw = (p[..., :o1], p[..., o1:o2], p[..., o2:o3],
                                          p[..., o3:o4], p[..., o4:o5], p[..., o5:])
        y_ssd = ssd_branch(z, xbc, dtf, dtb, ssd_conv_w[l], ssd_conv_b[l], dt_bias_f[l], dt_bias_b[l],
                           a_log_f[l], a_log_b[l], d_skip[l], ssd_norm_g[l])
        y_a = jnp.einsum('bsk,kd->bsd', y_ssd, w_ssd_out[l])
        y_b = jnp.einsum('bsk,kd->bsd', short_conv_branch(sc_in, sc_conv_w[l]), w_sc_out[l])
        gates = jax.nn.sigmoid(g_raw + b_gate[l])
        merged = gates[..., :D_MODEL] * y_a + gates[..., D_MODEL:] * y_b
        x = x + jnp.einsum('bsd,de->bse', merged, w_o[l])
        n2 = rms_norm(x, ffn_norm_g[l])
        hid = jax.nn.silu(jnp.einsum('bsd,df->bsf', n2, w_ffn_gate[l])) * jnp.einsum('bsd,df->bsf', n2, w_ffn_up[l])
        x = x + jnp.einsum('bsf,fd->bsd', hid, w_ffn_down[l])
    return rms_norm(x, final_norm_g)
```

```python
import functools

import jax
import jax.numpy as jnp
from jax import lax
from jax.experimental import pallas as pl
from jax.experimental.pallas import tpu as pltpu

F32 = jnp.float32
BF16 = jnp.bfloat16

D_MODEL = 4096
SEQ = 8192
DEPTH = 4
HEAD_DIM = 64
D_SSD = D_MODEL // 2
HEADS = D_SSD // HEAD_DIM
GROUPS = 4
HEADS_PER_GROUP = HEADS // GROUPS
STATE = 128
GROUP_WIDTH = D_SSD // GROUPS
SSD_CONV = 5
CHUNK = 128
N_CHUNKS = SEQ // CHUNK
XBC_DIM = D_SSD + 2 * GROUPS * STATE
D_SC = D_MODEL // 4
SC_CONV = 3
D_FF = 11008
EPS = 1e-6

LANES = 128
BF16_SUBLANES = 16
VMEM_LIMIT_BYTES = 56 * 1024 * 1024

D_FF_PAD = 11264
DT_COLS = LANES
COL_Z = 0
COL_XBC = D_SSD
COL_SC = COL_XBC + XBC_DIM
COL_GATE = COL_SC + 3 * D_SC
MAIN_COLS = COL_GATE + 2 * D_MODEL

BM = 1024
BN = 1024


def _params(*sem):
    return pltpu.CompilerParams(dimension_semantics=sem, vmem_limit_bytes=VMEM_LIMIT_BYTES)


def _rmsnorm_kernel(x_ref, g_ref, o_ref):
    x = x_ref[...]
    y = x * lax.rsqrt(jnp.mean(x * x, axis=-1, keepdims=True) + EPS)
    o_ref[...] = (y * g_ref[...]).astype(o_ref.dtype)


def rmsnorm(x, g, out_dtype, rows=256):
    m, d = x.shape
    return pl.pallas_call(
        _rmsnorm_kernel,
        out_shape=jax.ShapeDtypeStruct((m, d), out_dtype),
        grid=(m // rows,),
        in_specs=[pl.BlockSpec((rows, d), lambda i: (i, 0)),
                  pl.BlockSpec((1, d), lambda i: (0, 0))],
        out_specs=pl.BlockSpec((rows, d), lambda i: (i, 0)),
        compiler_params=_params("parallel"),
        name="rmsnorm",
    )(x, g.reshape(1, d))


def _inproj_kernel(a_ref, w_ref, b_ref, o_ref, *, gate_block0):
    j = pl.program_id(1)
    acc = jnp.dot(a_ref[...], w_ref[...], preferred_element_type=F32)

    @pl.when(j < gate_block0)
    def _():
        o_ref[...] = acc.astype(o_ref.dtype)

    @pl.when(j >= gate_block0)
    def _():
        o_ref[...] = jax.nn.sigmoid(acc + b_ref[...]).astype(o_ref.dtype)


def inproj_main(n, w_main, b_gate, layer):
    m, k = n.shape
    gate_block0 = COL_GATE // BN
    return pl.pallas_call(
        functools.partial(_inproj_kernel, gate_block0=gate_block0),
        out_shape=jax.ShapeDtypeStruct((m, MAIN_COLS), BF16),
        grid=(m // BM, MAIN_COLS // BN),
        in_specs=[pl.BlockSpec((BM, k), lambda i, j: (i, 0)),
                  pl.BlockSpec((None, k, BN), lambda i, j: (layer, 0, j)),
                  pl.BlockSpec((1, BN), lambda i, j: (0, jnp.maximum(j - gate_block0, 0)))],
        out_specs=pl.BlockSpec((BM, BN), lambda i, j: (i, j)),
        compiler_params=_params("parallel", "parallel"),
        name="inproj_main",
    )(n, w_main, b_gate.reshape(1, -1))


def _mm_kernel(a_ref, w_ref, o_ref):
    o_ref[...] = jnp.dot(a_ref[...], w_ref[...], preferred_element_type=F32).astype(o_ref.dtype)


def inproj_dt(n, w_dt, layer):
    m, k = n.shape
    return pl.pallas_call(
        _mm_kernel,
        out_shape=jax.ShapeDtypeStruct((m, DT_COLS), F32),
        grid=(m // BM,),
        in_specs=[pl.BlockSpec((BM, k), lambda i: (i, 0)),
                  pl.BlockSpec((None, k, DT_COLS), lambda i: (layer, 0, 0))],
        out_specs=pl.BlockSpec((BM, DT_COLS), lambda i: (i, 0)),
        compiler_params=_params("parallel"),
        name="inproj_dt",
    )(n, w_dt)


CONV_ROWS = 512
CONV_COLS = 256
HALO = BF16_SUBLANES


def _with_halo(prev_ref, cur_ref, next_ref):
    i = pl.program_id(1)
    last = pl.num_programs(1) - 1
    prev = jnp.where(i == 0, 0.0, prev_ref[...].astype(F32))
    nxt = jnp.where(i == last, 0.0, next_ref[...].astype(F32))
    return jnp.concatenate([prev, cur_ref[...].astype(F32), nxt], axis=0)


def _centred_conv(u, w, width):
    pad = width // 2
    out = u[HALO - pad:HALO - pad + CONV_ROWS, :] * w[0:1, :]
    for t in range(1, width):
        out = out + u[HALO - pad + t:HALO - pad + t + CONV_ROWS, :] * w[t:t + 1, :]
    return out


def _halo_specs(col_block0):
    r = CONV_ROWS // HALO
    n_halo_blocks = SEQ // HALO
    return [
        pl.BlockSpec((HALO, CONV_COLS), lambda j, i: (jnp.maximum(i * r - 1, 0), col_block0 + j)),
        pl.BlockSpec((CONV_ROWS, CONV_COLS), lambda j, i: (i, col_block0 + j)),
        pl.BlockSpec((HALO, CONV_COLS),
                     lambda j, i: (jnp.minimum((i + 1) * r, n_halo_blocks - 1), col_block0 + j)),
    ]


def _ssd_conv_kernel(prev_ref, cur_ref, next_ref, w_ref, b_ref, o_ref):
    u = _with_halo(prev_ref, cur_ref, next_ref)
    o_ref[...] = jax.nn.silu(_centred_conv(u, w_ref[...], SSD_CONV) + b_ref[...]).astype(o_ref.dtype)


def ssd_conv(p_main, conv_w, conv_b):
    return pl.pallas_call(
        _ssd_conv_kernel,
        out_shape=jax.ShapeDtypeStruct((SEQ, XBC_DIM), BF16),
        grid=(XBC_DIM // CONV_COLS, SEQ // CONV_ROWS),
        in_specs=_halo_specs(COL_XBC // CONV_COLS) + [
            pl.BlockSpec((SSD_CONV, CONV_COLS), lambda j, i: (0, j)),
            pl.BlockSpec((1, CONV_COLS), lambda j, i: (0, j))],
        out_specs=pl.BlockSpec((CONV_ROWS, CONV_COLS), lambda j, i: (i, j)),
        compiler_params=_params("parallel", "parallel"),
        name="ssd_conv",
    )(p_main, p_main, p_main, conv_w, conv_b.reshape(1, -1))


def _short_conv_kernel(b_ref, cprev_ref, c_ref, cnext_ref, hprev_ref, h_ref, hnext_ref, w_ref, o_ref):
    u = _with_halo(cprev_ref, c_ref, cnext_ref) * _with_halo(hprev_ref, h_ref, hnext_ref)
    o_ref[...] = (b_ref[...].astype(F32) * _centred_conv(u, w_ref[...], SC_CONV)).astype(o_ref.dtype)


def short_conv(p_main, conv_w):
    b0 = COL_SC // CONV_COLS
    c0 = (COL_SC + D_SC) // CONV_COLS
    h0 = (COL_SC + 2 * D_SC) // CONV_COLS
    return pl.pallas_call(
        _short_conv_kernel,
        out_shape=jax.ShapeDtypeStruct((SEQ, D_SC), BF16),
        grid=(D_SC // CONV_COLS, SEQ // CONV_ROWS),
        in_specs=[pl.BlockSpec((CONV_ROWS, CONV_COLS), lambda j, i: (i, b0 + j))]
        + _halo_specs(c0) + _halo_specs(h0)
        + [pl.BlockSpec((SC_CONV, CONV_COLS), lambda j, i: (0, j))],
        out_specs=pl.BlockSpec((CONV_ROWS, CONV_COLS), lambda j, i: (i, j)),
        compiler_params=_params("parallel", "parallel"),
        name="short_conv",
    )(p_main, p_main, p_main, p_main, p_main, p_main, p_main, conv_w)


def _expand_heads(v, e_hi):
    hi = v.astype(BF16)
    lo = (v - hi.astype(F32)).astype(BF16)
    return (jnp.dot(hi, e_hi, preferred_element_type=F32)
            + jnp.dot(lo, e_hi, preferred_element_type=F32))


def _ssd_chunk(xs_ref, b_ref, c_ref, dt_ref, dtb_ref, alog_ref, e_ref, state_ref, *, reverse):
    off = HEADS if reverse else 0
    row = lax.broadcasted_iota(jnp.int32, (CHUNK, CHUNK), 0)
    col = lax.broadcasted_iota(jnp.int32, (CHUNK, CHUNK), 1)
    mask = (row <= col) if reverse else (row >= col)
    tri = mask.astype(F32)

    dt = jax.nn.softplus(dt_ref[...] + dtb_ref[...])
    a = dt * (-jnp.exp(alog_ref[...]))
    g_all = jnp.dot(tri, a, preferred_element_type=F32, precision=lax.Precision.HIGHEST)
    g_tot = g_all[0:1, :] if reverse else g_all[CHUNK - 1:CHUNK, :]
    g_t = g_all.T
    dt_t = dt.T

    xs = xs_ref[...]
    state = state_ref[...]
    state_bf = state.astype(BF16)

    ys = []
    for grp in range(GROUPS):
        b_g = b_ref[:, grp * STATE:(grp + 1) * STATE]
        c_g = c_ref[:, grp * STATE:(grp + 1) * STATE]
        cb = lax.dot_general(c_g, b_g, (((1,), (1,)), ((), ())), preferred_element_type=F32)
        c_f = c_g.astype(F32)
        for r in range(HEADS_PER_GROUP):
            h = grp * HEADS_PER_GROUP + r
            g_col = g_all[:, off + h:off + h + 1]
            g_row = g_t[off + h:off + h + 1, :]
            diff = g_col - g_row
            lmat = jnp.where(mask, jnp.exp(jnp.where(mask, diff, 0.0)), 0.0)
            m_h = (cb * lmat * dt_t[off + h:off + h + 1, :]).astype(BF16)
            c_s = (c_f * jnp.exp(g_col)).astype(BF16)
            lhs = jnp.concatenate([m_h, c_s], axis=1)
            sl = slice(h * HEAD_DIM, (h + 1) * HEAD_DIM)
            rhs = jnp.concatenate([xs[:, sl], state_bf[:, sl]], axis=0)
            ys.append(jnp.dot(lhs, rhs, preferred_element_type=F32))
    y = jnp.concatenate(ys, axis=1)

    e_hi = e_ref[...]
    w = dt * jnp.exp(g_tot - g_all)
    xw = (xs.astype(F32) * _expand_heads(w, e_hi)).astype(BF16)
    decay = _expand_heads(jnp.broadcast_to(jnp.exp(g_tot), (8, DT_COLS)), e_hi)[0:1, :]
    new = []
    for grp in range(GROUPS):
        b_g = b_ref[:, grp * STATE:(grp + 1) * STATE]
        new.append(lax.dot_general(b_g, xw[:, grp * GROUP_WIDTH:(grp + 1) * GROUP_WIDTH],
                                   (((0,), (0,)), ((), ())), preferred_element_type=F32))
    state_ref[...] = state * decay + jnp.concatenate(new, axis=1)
    return y


def _ssd_fwd_kernel(xs_ref, b_ref, c_ref, dt_ref, dtb_ref, alog_ref, e_ref, y_ref, state_ref):
    @pl.when(pl.program_id(0) == 0)
    def _():
        state_ref[...] = jnp.zeros_like(state_ref)

    y_ref[...] = _ssd_chunk(xs_ref, b_ref, c_ref, dt_ref, dtb_ref, alog_ref, e_ref, state_ref,
                            reverse=False)


def _ssd_bwd_kernel(xs_ref, b_ref, c_ref, dt_ref, dtb_ref, alog_ref, e_ref, yf_ref, z_ref,
                    dskip_ref, ng_ref, o_ref, state_ref):
    @pl.when(pl.program_id(0) == 0)
    def _():
        state_ref[...] = jnp.zeros_like(state_ref)

    y = _ssd_chunk(xs_ref, b_ref, c_ref, dt_ref, dtb_ref, alog_ref, e_ref, state_ref, reverse=True)
    y = y + yf_ref[...] + dskip_ref[...] * xs_ref[...].astype(F32)
    y = y * jax.nn.silu(z_ref[...].astype(F32))
    ng = ng_ref[...]
    outs = []
    for grp in range(GROUPS):
        sl = slice(grp * GROUP_WIDTH, (grp + 1) * GROUP_WIDTH)
        yg = y[:, sl]
        yg = yg * lax.rsqrt(jnp.mean(yg * yg, axis=-1, keepdims=True) + EPS)
        outs.append(yg * ng[:, sl])
    o_ref[...] = jnp.concatenate(outs, axis=1).astype(o_ref.dtype)


def ssd_scan(xbc, dt_raw, p_main, dt_bias, a_log, e_sel_f, e_sel_b, d_skip_x, norm_g):
    gn = GROUPS * STATE
    b_blk = D_SSD // gn
    common = lambda cm: [
        pl.BlockSpec((CHUNK, D_SSD), lambda c: (cm(c), 0)),
        pl.BlockSpec((CHUNK, gn), lambda c: (cm(c), b_blk)),
        pl.BlockSpec((CHUNK, gn), lambda c: (cm(c), b_blk + 1)),
        pl.BlockSpec((CHUNK, DT_COLS), lambda c: (cm(c), 0)),
        pl.BlockSpec((1, DT_COLS), lambda c: (0, 0)),
        pl.BlockSpec((1, DT_COLS), lambda c: (0, 0)),
        pl.BlockSpec((DT_COLS, D_SSD), lambda c: (0, 0)),
    ]
    fwd = lambda c: c
    bwd = lambda c: N_CHUNKS - 1 - c
    y_f = pl.pallas_call(
        _ssd_fwd_kernel,
        out_shape=jax.ShapeDtypeStruct((SEQ, D_SSD), F32),
        grid=(N_CHUNKS,),
        in_specs=common(fwd),
        out_specs=pl.BlockSpec((CHUNK, D_SSD), lambda c: (c, 0)),
        scratch_shapes=[pltpu.VMEM((STATE, D_SSD), F32)],
        compiler_params=_params("arbitrary"),
        name="ssd_fwd",
    )(xbc, xbc, xbc, dt_raw, dt_bias, a_log, e_sel_f)
    return pl.pallas_call(
        _ssd_bwd_kernel,
        out_shape=jax.ShapeDtypeStruct((SEQ, D_SSD), BF16),
        grid=(N_CHUNKS,),
        in_specs=common(bwd) + [
            pl.BlockSpec((CHUNK, D_SSD), lambda c: (bwd(c), 0)),
            pl.BlockSpec((CHUNK, D_SSD), lambda c: (bwd(c), 0)),
            pl.BlockSpec((1, D_SSD), lambda c: (0, 0)),
            pl.BlockSpec((1, D_SSD), lambda c: (0, 0)),
        ],
        out_specs=pl.BlockSpec((CHUNK, D_SSD), lambda c: (bwd(c), 0)),
        scratch_shapes=[pltpu.VMEM((STATE, D_SSD), F32)],
        compiler_params=_params("arbitrary"),
        name="ssd_bwd",
    )(xbc, xbc, xbc, dt_raw, dt_bias, a_log, e_sel_b, y_f, p_main, d_skip_x, norm_g)


def _merge_kernel(ya_ref, sc_ref, wa_ref, wb_ref, ga_ref, gb_ref, o_ref):
    y_a = jnp.dot(ya_ref[...], wa_ref[...], preferred_element_type=F32)
    y_b = jnp.dot(sc_ref[...], wb_ref[...], preferred_element_type=F32)
    o_ref[...] = (ga_ref[...].astype(F32) * y_a + gb_ref[...].astype(F32) * y_b).astype(o_ref.dtype)


def merge_branches(y_ssd, sc, w_a, w_b, p_main, layer):
    ga0 = COL_GATE // BN
    gb0 = (COL_GATE + D_MODEL) // BN
    return pl.pallas_call(
        _merge_kernel,
        out_shape=jax.ShapeDtypeStruct((SEQ, D_MODEL), BF16),
        grid=(SEQ // BM, D_MODEL // BN),
        in_specs=[pl.BlockSpec((BM, D_SSD), lambda i, j: (i, 0)),
                  pl.BlockSpec((BM, D_SC), lambda i, j: (i, 0)),
                  pl.BlockSpec((None, D_SSD, BN), lambda i, j: (layer, 0, j)),
                  pl.BlockSpec((None, D_SC, BN), lambda i, j: (layer, 0, j)),
                  pl.BlockSpec((BM, BN), lambda i, j: (i, ga0 + j)),
                  pl.BlockSpec((BM, BN), lambda i, j: (i, gb0 + j))],
        out_specs=pl.BlockSpec((BM, BN), lambda i, j: (i, j)),
        compiler_params=_params("parallel", "parallel"),
        name="merge_branches",
    )(y_ssd, sc, w_a, w_b, p_main, p_main)


def _mm_residual_kernel(a_ref, w_ref, x_ref, o_ref):
    o_ref[...] = x_ref[...] + jnp.dot(a_ref[...], w_ref[...], preferred_element_type=F32)


def matmul_residual(a, w, x, layer, bn=512):
    m, k = a.shape
    n = w.shape[2]
    return pl.pallas_call(
        _mm_residual_kernel,
        out_shape=jax.ShapeDtypeStruct((m, n), F32),
        grid=(m // BM, n // bn),
        in_specs=[pl.BlockSpec((BM, k), lambda i, j: (i, 0)),
                  pl.BlockSpec((None, k, bn), lambda i, j: (layer, 0, j)),
                  pl.BlockSpec((BM, bn), lambda i, j: (i, j))],
        out_specs=pl.BlockSpec((BM, bn), lambda i, j: (i, j)),
        compiler_params=_params("parallel", "parallel"),
        name="matmul_residual",
    )(a, w, x)


def _ffn_up_kernel(a_ref, wg_ref, wu_ref, o_ref):
    a = a_ref[...]
    gate = jnp.dot(a, wg_ref[...], preferred_element_type=F32)
    up = jnp.dot(a, wu_ref[...], preferred_element_type=F32)
    o_ref[...] = (jax.nn.silu(gate) * up).astype(o_ref.dtype)


def ffn_up(n2, w_gate, w_up, layer, bn=512):
    m, k = n2.shape
    return pl.pallas_call(
        _ffn_up_kernel,
        out_shape=jax.ShapeDtypeStruct((m, D_FF_PAD), BF16),
        grid=(m // BM, D_FF_PAD // bn),
        in_specs=[pl.BlockSpec((BM, k), lambda i, j: (i, 0)),
                  pl.BlockSpec((None, k, bn), lambda i, j: (layer, 0, j)),
                  pl.BlockSpec((None, k, bn), lambda i, j: (layer, 0, j))],
        out_specs=pl.BlockSpec((BM, bn), lambda i, j: (i, j)),
        compiler_params=_params("parallel", "parallel"),
        name="ffn_up",
    )(n2, w_gate, w_up)


FFN_DOWN_TK = D_FF_PAD // 4


def _ffn_down_kernel(a_ref, w_ref, x_ref, o_ref, acc_ref):
    k = pl.program_id(2)

    @pl.when(k == 0)
    def _():
        acc_ref[...] = x_ref[...]

    acc_ref[...] += jnp.dot(a_ref[...], w_ref[...], preferred_element_type=F32)

    @pl.when(k == pl.num_programs(2) - 1)
    def _():
        o_ref[...] = acc_ref[...]


def ffn_down(hid, w_down, x, layer):
    m = hid.shape[0]
    n = w_down.shape[2]
    tk = FFN_DOWN_TK
    return pl.pallas_call(
        _ffn_down_kernel,
        out_shape=jax.ShapeDtypeStruct((m, n), F32),
        grid=(m // BM, n // BN, D_FF_PAD // tk),
        in_specs=[pl.BlockSpec((BM, tk), lambda i, j, k: (i, k)),
                  pl.BlockSpec((None, tk, BN), lambda i, j, k: (layer, k, j)),
                  pl.BlockSpec((BM, BN), lambda i, j, k: (i, j))],
        out_specs=pl.BlockSpec((BM, BN), lambda i, j, k: (i, j)),
        scratch_shapes=[pltpu.VMEM((BM, BN), F32)],
        compiler_params=_params("parallel", "parallel", "arbitrary"),
        name="ffn_down",
    )(hid, w_down, x)


def _head_selection(offset):
    rows = jnp.arange(DT_COLS)[:, None]
    heads = jnp.arange(D_SSD)[None, :] // HEAD_DIM
    return (rows == heads + offset).astype(BF16)


def _pad_lanes(v, width):
    return jnp.pad(v, ((0, 0), (0, width - v.shape[1])))


def kernel(x, mix_norm_g, w_in, b_gate, ssd_conv_w, ssd_conv_b, dt_bias_f, dt_bias_b, a_log_f, a_log_b, d_skip, ssd_norm_g, w_ssd_out, sc_conv_w, w_sc_out, w_o, ffn_norm_g, w_ffn_gate, w_ffn_up, w_ffn_down, final_norm_g):
    o_xbc_end = D_SSD + XBC_DIM
    o_dt_end = o_xbc_end + 2 * HEADS
    w_main = jnp.concatenate([w_in[:, :, :o_xbc_end], w_in[:, :, o_dt_end:]], axis=2).astype(BF16)
    w_dt = jnp.pad(w_in[:, :, o_xbc_end:o_dt_end], ((0, 0), (0, 0), (0, DT_COLS - 2 * HEADS))).astype(BF16)
    w_a = w_ssd_out.astype(BF16)
    w_b = w_sc_out.astype(BF16)
    w_o_bf = w_o.astype(BF16)
    ff_pad = D_FF_PAD - D_FF
    w_gate = jnp.pad(w_ffn_gate, ((0, 0), (0, 0), (0, ff_pad))).astype(BF16)
    w_up = jnp.pad(w_ffn_up, ((0, 0), (0, 0), (0, ff_pad))).astype(BF16)
    w_down = jnp.pad(w_ffn_down, ((0, 0), (0, ff_pad), (0, 0))).astype(BF16)
    dt_bias = _pad_lanes(jnp.concatenate([dt_bias_f, dt_bias_b], axis=1), DT_COLS)
    a_log = _pad_lanes(jnp.concatenate([a_log_f, a_log_b], axis=1), DT_COLS)
    d_skip_x = jnp.repeat(d_skip, HEAD_DIM, axis=1)
    e_sel_f = _head_selection(0)
    e_sel_b = _head_selection(HEADS)

    h = x.reshape(SEQ, D_MODEL)
    for l in range(DEPTH):
        n = rmsnorm(h, mix_norm_g[l], BF16)
        p_main = inproj_main(n, w_main, b_gate[l], l)
        dt_raw = inproj_dt(n, w_dt, l)
        xbc = ssd_conv(p_main, ssd_conv_w[l], ssd_conv_b[l])
        sc = short_conv(p_main, sc_conv_w[l])
        y_ssd = ssd_scan(xbc, dt_raw, p_main, dt_bias[l:l + 1], a_log[l:l + 1], e_sel_f, e_sel_b,
                         d_skip_x[l:l + 1], ssd_norm_g[l:l + 1])
        merged = merge_branches(y_ssd, sc, w_a, w_b, p_main, l)
        h = matmul_residual(merged, w_o_bf, h, l)
        n2 = rmsnorm(h, ffn_norm_g[l], BF16)
        hid = ffn_up(n2, w_gate, w_up, l)
        h = ffn_down(hid, w_down, h, l)
    out = rmsnorm(h, final_norm_g, F32)
    return out.reshape(x.shape)
```

```python
import functools

import jax
import jax.numpy as jnp
from jax import lax
from jax.experimental import pallas as pl
from jax.experimental.pallas import tpu as pltpu

F32 = jnp.float32
BF16 = jnp.bfloat16

D_MODEL = 4096
SEQ = 8192
DEPTH = 4
HEAD_DIM = 64
D_SSD = D_MODEL // 2
HEADS = D_SSD // HEAD_DIM
GROUPS = 4
HEADS_PER_GROUP = HEADS // GROUPS
STATE = 128
GROUP_WIDTH = D_SSD // GROUPS
SSD_CONV = 5
CHUNK = 128
N_CHUNKS = SEQ // CHUNK
XBC_DIM = D_SSD + 2 * GROUPS * STATE
D_SC = D_MODEL // 4
SC_CONV = 3
D_FF = 11008
EPS = 1e-6
LOG2_E = 1.4426950408889634

LANES = 128
BF16_SUBLANES = 16
VMEM_LIMIT_BYTES = 56 * 1024 * 1024

W_IN_DT = D_SSD + XBC_DIM
W_IN_SC = W_IN_DT + 2 * HEADS
DT_COLS = LANES
ZX_COLS = D_SSD + XBC_DIM
SG_GATE = 3 * D_SC
SG_COLS = SG_GATE + 2 * D_MODEL

FF_BN = 256
FF_BLOCKS = D_FF // FF_BN
D_FF_PAD = 11264
FFN_DOWN_TK = D_FF_PAD // 4

BM = 1024
BN = 1024
MM_ROWS = 256


def _params(*sem):
    return pltpu.CompilerParams(dimension_semantics=sem, vmem_limit_bytes=VMEM_LIMIT_BYTES)


def _row_scale(ssq_ref, width):
    r = lax.rsqrt(ssq_ref[...] * (1.0 / D_MODEL) + EPS)
    return jnp.concatenate([r] * (width // LANES), axis=1) if width > LANES else r


def _sum_squares(o, like_ref):
    return jnp.broadcast_to(jnp.sum(o * o, axis=-1, keepdims=True), like_ref.shape)


def _norm_prep_kernel(x_ref, g_ref, xg_ref, ssq_ref):
    x = x_ref[...]
    xg_ref[...] = (x * g_ref[...]).astype(xg_ref.dtype)
    ssq_ref[...] = _sum_squares(x, ssq_ref)


def norm_prep(x, g, rows=256):
    m, d = x.shape
    return pl.pallas_call(
        _norm_prep_kernel,
        out_shape=(jax.ShapeDtypeStruct((m, d), BF16), jax.ShapeDtypeStruct((m, LANES), F32)),
        grid=(m // rows,),
        in_specs=[pl.BlockSpec((rows, d), lambda i: (i, 0)),
                  pl.BlockSpec((1, d), lambda i: (0, 0))],
        out_specs=(pl.BlockSpec((rows, d), lambda i: (i, 0)),
                   pl.BlockSpec((rows, LANES), lambda i: (i, 0))),
        compiler_params=_params("parallel"),
        name="norm_prep",
    )(x, g.reshape(1, d))


def _rmsnorm_kernel(x_ref, g_ref, o_ref):
    x = x_ref[...]
    y = x * lax.rsqrt(jnp.mean(x * x, axis=-1, keepdims=True) + EPS)
    o_ref[...] = (y * g_ref[...]).astype(o_ref.dtype)


def rmsnorm(x, g, out_dtype, rows=256):
    m, d = x.shape
    return pl.pallas_call(
        _rmsnorm_kernel,
        out_shape=jax.ShapeDtypeStruct((m, d), out_dtype),
        grid=(m // rows,),
        in_specs=[pl.BlockSpec((rows, d), lambda i: (i, 0)),
                  pl.BlockSpec((1, d), lambda i: (0, 0))],
        out_specs=pl.BlockSpec((rows, d), lambda i: (i, 0)),
        compiler_params=_params("parallel"),
        name="rmsnorm",
    )(x, g.reshape(1, d))


def _row_groups(n_rows):
    return [slice(r, r + MM_ROWS) for r in range(0, n_rows, MM_ROWS)]


def _scaled_mm_kernel(a_ref, ssq_ref, w_ref, o_ref):
    w = w_ref[...].astype(BF16)
    for rows in _row_groups(a_ref.shape[0]):
        acc = jnp.dot(a_ref[rows, :], w, preferred_element_type=F32)
        o_ref[rows, :] = (acc * _row_scale(ssq_ref.at[rows, :], o_ref.shape[1])).astype(o_ref.dtype)


def inproj_cols(xg, ssq, w_in, layer, col_block0, n_cols, bn, out_dtype, name):
    m, k = xg.shape
    return pl.pallas_call(
        _scaled_mm_kernel,
        out_shape=jax.ShapeDtypeStruct((m, n_cols), out_dtype),
        grid=(m // BM, n_cols // bn),
        in_specs=[pl.BlockSpec((BM, k), lambda i, j: (i, 0)),
                  pl.BlockSpec((BM, LANES), lambda i, j: (i, 0)),
                  pl.BlockSpec((None, k, bn), lambda i, j: (layer, 0, col_block0 + j))],
        out_specs=pl.BlockSpec((BM, bn), lambda i, j: (i, j)),
        compiler_params=_params("parallel", "parallel"),
        name=name,
    )(xg, ssq, w_in)


def _inproj_sg_kernel(a_ref, ssq_ref, w_ref, b_ref, o_ref, *, gate_block0):
    is_gate = pl.program_id(1) >= gate_block0
    w = w_ref[...]
    for rows in _row_groups(a_ref.shape[0]):
        acc = jnp.dot(a_ref[rows, :], w, preferred_element_type=F32)
        v = acc * _row_scale(ssq_ref.at[rows, :], o_ref.shape[1])
        o_ref[rows, :] = jnp.where(is_gate, jax.nn.sigmoid(v + b_ref[...]), v).astype(o_ref.dtype)


def inproj_sg(xg, ssq, w_sg, b_gate, layer):
    m, k = xg.shape
    gate_block0 = SG_GATE // BN
    return pl.pallas_call(
        functools.partial(_inproj_sg_kernel, gate_block0=gate_block0),
        out_shape=jax.ShapeDtypeStruct((m, SG_COLS), BF16),
        grid=(m // BM, SG_COLS // BN),
        in_specs=[pl.BlockSpec((BM, k), lambda i, j: (i, 0)),
                  pl.BlockSpec((BM, LANES), lambda i, j: (i, 0)),
                  pl.BlockSpec((None, k, BN), lambda i, j: (layer, 0, j)),
                  pl.BlockSpec((1, BN), lambda i, j: (0, jnp.maximum(j - gate_block0, 0)))],
        out_specs=pl.BlockSpec((BM, BN), lambda i, j: (i, j)),
        compiler_params=_params("parallel", "parallel"),
        name="inproj_sg",
    )(xg, ssq, w_sg, b_gate.reshape(1, -1))


CONV_ROWS = 512
CONV_COLS = 256
HALO = BF16_SUBLANES


def _with_halo(prev_ref, cur_ref, next_ref):
    i = pl.program_id(1)
    last = pl.num_programs(1) - 1
    prev = jnp.where(i == 0, 0.0, prev_ref[...].astype(F32))
    nxt = jnp.where(i == last, 0.0, next_ref[...].astype(F32))
    return jnp.concatenate([prev, cur_ref[...].astype(F32), nxt], axis=0)


def _centred_conv(u, w, width):
    pad = width // 2
    out = u[HALO - pad:HALO - pad + CONV_ROWS, :] * w[0:1, :]
    for t in range(1, width):
        out = out + u[HALO - pad + t:HALO - pad + t + CONV_ROWS, :] * w[t:t + 1, :]
    return out


def _halo_specs(col_block0):
    r = CONV_ROWS // HALO
    n_halo_blocks = SEQ // HALO
    return [
        pl.BlockSpec((HALO, CONV_COLS), lambda j, i: (jnp.maximum(i * r - 1, 0), col_block0 + j)),
        pl.BlockSpec((CONV_ROWS, CONV_COLS), lambda j, i: (i, col_block0 + j)),
        pl.BlockSpec((HALO, CONV_COLS),
                     lambda j, i: (jnp.minimum((i + 1) * r, n_halo_blocks - 1), col_block0 + j)),
    ]


def _ssd_conv_kernel(prev_ref, cur_ref, next_ref, w_ref, b_ref, o_ref):
    u = _with_halo(prev_ref, cur_ref, next_ref)
    o_ref[...] = jax.nn.silu(_centred_conv(u, w_ref[...], SSD_CONV) + b_ref[...]).astype(o_ref.dtype)


def ssd_conv(p_zx, conv_w, conv_b):
    return pl.pallas_call(
        _ssd_conv_kernel,
        out_shape=jax.ShapeDtypeStruct((SEQ, XBC_DIM), BF16),
        grid=(XBC_DIM // CONV_COLS, SEQ // CONV_ROWS),
        in_specs=_halo_specs(D_SSD // CONV_COLS) + [
            pl.BlockSpec((SSD_CONV, CONV_COLS), lambda j, i: (0, j)),
            pl.BlockSpec((1, CONV_COLS), lambda j, i: (0, j))],
        out_specs=pl.BlockSpec((CONV_ROWS, CONV_COLS), lambda j, i: (i, j)),
        compiler_params=_params("parallel", "parallel"),
        name="ssd_conv",
    )(p_zx, p_zx, p_zx, conv_w, conv_b.reshape(1, -1))


def _short_conv_kernel(b_ref, cprev_ref, c_ref, cnext_ref, hprev_ref, h_ref, hnext_ref, w_ref, o_ref):
    u = _with_halo(cprev_ref, c_ref, cnext_ref) * _with_halo(hprev_ref, h_ref, hnext_ref)
    o_ref[...] = (b_ref[...].astype(F32) * _centred_conv(u, w_ref[...], SC_CONV)).astype(o_ref.dtype)


def short_conv(p_sg, conv_w):
    c0 = D_SC // CONV_COLS
    h0 = 2 * D_SC // CONV_COLS
    return pl.pallas_call(
        _short_conv_kernel,
        out_shape=jax.ShapeDtypeStruct((SEQ, D_SC), BF16),
        grid=(D_SC // CONV_COLS, SEQ // CONV_ROWS),
        in_specs=[pl.BlockSpec((CONV_ROWS, CONV_COLS), lambda j, i: (i, j))]
        + _halo_specs(c0) + _halo_specs(h0)
        + [pl.BlockSpec((SC_CONV, CONV_COLS), lambda j, i: (0, j))],
        out_specs=pl.BlockSpec((CONV_ROWS, CONV_COLS), lambda j, i: (i, j)),
        compiler_params=_params("parallel", "parallel"),
        name="short_conv",
    )(p_sg, p_sg, p_sg, p_sg, p_sg, p_sg, p_sg, conv_w)


def _expand_heads(v, e_hi):
    hi = v.astype(BF16)
    lo = (v - hi.astype(F32)).astype(BF16)
    return (jnp.dot(hi, e_hi, preferred_element_type=F32)
            + jnp.dot(lo, e_hi, preferred_element_type=F32))


PREP_CHUNKS = 8


def _ssd_prep_kernel(dt_ref, dtb_ref, alog_ref, g2_ref, srct_ref, w_ref, dout_ref):
    row = lax.broadcasted_iota(jnp.int32, (CHUNK, CHUNK), 0)
    col = lax.broadcasted_iota(jnp.int32, (CHUNK, CHUNK), 1)
    lower = (row >= col).astype(F32)
    upper = (row <= col).astype(F32)
    is_fwd = lax.broadcasted_iota(jnp.int32, (CHUNK, DT_COLS), 1) < HEADS
    a_neg = -jnp.exp(alog_ref[...])
    for c in range(PREP_CHUNKS):
        rows = slice(c * CHUNK, (c + 1) * CHUNK)
        dt = jax.nn.softplus(dt_ref[rows, :] + dtb_ref[...])
        a = dt * a_neg
        hp = lax.Precision.HIGHEST
        g = (jnp.dot(lower, jnp.where(is_fwd, a, 0.0), preferred_element_type=F32, precision=hp)
             + jnp.dot(upper, jnp.where(is_fwd, 0.0, a), preferred_element_type=F32, precision=hp))
        g2 = LOG2_E * g
        g2_tot = jnp.where(is_fwd[0:1, :], g2[CHUNK - 1:CHUNK, :], g2[0:1, :])
        g2_ref[rows, :] = g2
        srct_ref[rows, :] = (g2 - jnp.log2(dt)).T
        w_ref[rows, :] = dt * jnp.exp2(g2_tot - g2)
        dout_ref[c * 8:(c + 1) * 8, :] = jnp.broadcast_to(jnp.exp2(g2_tot), (8, DT_COLS))


def ssd_prep(dt_raw, dt_bias, a_log):
    rows = PREP_CHUNKS * CHUNK
    full = jax.ShapeDtypeStruct((SEQ, DT_COLS), F32)
    blk = pl.BlockSpec((rows, DT_COLS), lambda i: (i, 0))
    vec = pl.BlockSpec((1, DT_COLS), lambda i: (0, 0))
    return pl.pallas_call(
        _ssd_prep_kernel,
        out_shape=(full, full, full, jax.ShapeDtypeStruct((N_CHUNKS * 8, DT_COLS), F32)),
        grid=(SEQ // rows,),
        in_specs=[blk, vec, vec],
        out_specs=(blk, blk, blk, pl.BlockSpec((PREP_CHUNKS * 8, DT_COLS), lambda i: (i, 0))),
        compiler_params=_params("parallel"),
        name="ssd_prep",
    )(dt_raw, dt_bias, a_log)


def _ssd_chunk(xs_ref, b_ref, c_ref, g2_ref, srct_ref, w_ref, dout_ref, e_ref, state_ref, *, reverse):
    off = HEADS if reverse else 0
    row = lax.broadcasted_iota(jnp.int32, (CHUNK, CHUNK), 0)
    col = lax.broadcasted_iota(jnp.int32, (CHUNK, CHUNK), 1)
    mask = (row <= col) if reverse else (row >= col)
    g2 = g2_ref[...]
    src_t = srct_ref[...]

    xs = xs_ref[...]
    state = state_ref[...]
    state_bf = state.astype(BF16)
    lane_in_tile = lax.broadcasted_iota(jnp.int32, (CHUNK + STATE, LANES), 1)

    ys = []
    for grp in range(GROUPS):
        b_g = b_ref[:, grp * STATE:(grp + 1) * STATE]
        c_g = c_ref[:, grp * STATE:(grp + 1) * STATE]
        cb = lax.dot_general(c_g, b_g, (((1,), (1,)), ((), ())), preferred_element_type=F32)
        c_f = c_g.astype(F32)
        for r in range(HEADS_PER_GROUP):
            h = grp * HEADS_PER_GROUP + r
            g_col = jnp.broadcast_to(g2[:, off + h:off + h + 1], (CHUNK, CHUNK))
            lmat_dt = jnp.where(mask, jnp.exp2(g_col - src_t[off + h:off + h + 1, :]), 0.0)
            m_h = (cb * lmat_dt).astype(BF16)
            c_s = (c_f * jnp.exp2(g_col)).astype(BF16)
            lhs = jnp.concatenate([m_h, c_s], axis=1)
            pair = slice((h // 2) * LANES, (h // 2 + 1) * LANES)
            own = (lane_in_tile >= HEAD_DIM) if h % 2 else (lane_in_tile < HEAD_DIM)
            rhs = jnp.concatenate([xs[:, pair], state_bf[:, pair]], axis=0)
            part = jnp.dot(lhs, jnp.where(own, rhs, jnp.zeros_like(rhs)), preferred_element_type=F32)
            if h % 2:
                ys.append(ys.pop() + part)
            else:
                ys.append(part)
    y = jnp.concatenate(ys, axis=1)

    e_hi = e_ref[...]
    xw = (xs.astype(F32) * _expand_heads(w_ref[...], e_hi)).astype(BF16)
    decay = _expand_heads(dout_ref[...], e_hi)[0:1, :]
    new = []
    for grp in range(GROUPS):
        b_g = b_ref[:, grp * STATE:(grp + 1) * STATE]
        new.append(lax.dot_general(b_g, xw[:, grp * GROUP_WIDTH:(grp + 1) * GROUP_WIDTH],
                                   (((0,), (0,)), ((), ())), preferred_element_type=F32))
    state_ref[...] = state * decay + jnp.concatenate(new, axis=1)
    return y


def _ssd_fwd_kernel(*refs):
    *chunk_refs, y_ref, state_ref = refs

    @pl.when(pl.program_id(0) == 0)
    def _():
        state_ref[...] = jnp.zeros_like(state_ref)

    y_ref[...] = _ssd_chunk(*chunk_refs, state_ref, reverse=False)


def _ssd_bwd_kernel(*refs):
    *chunk_refs, yf_ref, z_ref, dskip_ref, ng_ref, o_ref, state_ref = refs
    xs_ref = chunk_refs[0]

    @pl.when(pl.program_id(0) == 0)
    def _():
        state_ref[...] = jnp.zeros_like(state_ref)

    y = _ssd_chunk(*chunk_refs, state_ref, reverse=True)
    y = y + yf_ref[...] + dskip_ref[...] * xs_ref[...].astype(F32)
    y = y * jax.nn.silu(z_ref[...].astype(F32))
    ng = ng_ref[...]
    outs = []
    for grp in range(GROUPS):
        sl = slice(grp * GROUP_WIDTH, (grp + 1) * GROUP_WIDTH)
        yg = y[:, sl]
        yg = yg * lax.rsqrt(jnp.mean(yg * yg, axis=-1, keepdims=True) + EPS)
        outs.append(yg * ng[:, sl])
    o_ref[...] = jnp.concatenate(outs, axis=1).astype(o_ref.dtype)


def ssd_scan(xbc, dt_raw, p_zx, dt_bias, a_log, e_sel_f, e_sel_b, d_skip_x, norm_g):
    gn = GROUPS * STATE
    b_blk = D_SSD // gn
    decays = ssd_prep(dt_raw, dt_bias, a_log)
    per_chunk = lambda cm: pl.BlockSpec((CHUNK, DT_COLS), lambda c: (cm(c), 0))
    common = lambda cm: [
        pl.BlockSpec((CHUNK, D_SSD), lambda c: (cm(c), 0)),
        pl.BlockSpec((CHUNK, gn), lambda c: (cm(c), b_blk)),
        pl.BlockSpec((CHUNK, gn), lambda c: (cm(c), b_blk + 1)),
        per_chunk(cm), per_chunk(cm), per_chunk(cm),
        pl.BlockSpec((8, DT_COLS), lambda c: (cm(c), 0)),
        pl.BlockSpec((DT_COLS, D_SSD), lambda c: (0, 0)),
    ]
    fwd = lambda c: c
    bwd = lambda c: N_CHUNKS - 1 - c
    y_f = pl.pallas_call(
        _ssd_fwd_kernel,
        out_shape=jax.ShapeDtypeStruct((SEQ, D_SSD), F32),
        grid=(N_CHUNKS,),
        in_specs=common(fwd),
        out_specs=pl.BlockSpec((CHUNK, D_SSD), lambda c: (c, 0)),
        scratch_shapes=[pltpu.VMEM((STATE, D_SSD), F32)],
        compiler_params=_params("arbitrary"),
        name="ssd_fwd",
    )(xbc, xbc, xbc, *decays, e_sel_f)
    return pl.pallas_call(
        _ssd_bwd_kernel,
        out_shape=jax.ShapeDtypeStruct((SEQ, D_SSD), BF16),
        grid=(N_CHUNKS,),
        in_specs=common(bwd) + [
            pl.BlockSpec((CHUNK, D_SSD), lambda c: (bwd(c), 0)),
            pl.BlockSpec((CHUNK, D_SSD), lambda c: (bwd(c), 0)),
            pl.BlockSpec((1, D_SSD), lambda c: (0, 0)),
            pl.BlockSpec((1, D_SSD), lambda c: (0, 0)),
        ],
        out_specs=pl.BlockSpec((CHUNK, D_SSD), lambda c: (bwd(c), 0)),
        scratch_shapes=[pltpu.VMEM((STATE, D_SSD), F32)],
        compiler_params=_params("arbitrary"),
        name="ssd_bwd",
    )(xbc, xbc, xbc, *decays, e_sel_b, y_f, p_zx, d_skip_x, norm_g)


MERGE_BN = 512


def _merge_kernel(ya_ref, sc_ref, wa_ref, wb_ref, ga_ref, gb_ref, o_ref):
    w_a = wa_ref[...].astype(BF16)
    w_b = wb_ref[...].astype(BF16)
    for rows in _row_groups(ya_ref.shape[0]):
        y_a = jnp.dot(ya_ref[rows, :], w_a, preferred_element_type=F32)
        y_b = jnp.dot(sc_ref[rows, :], w_b, preferred_element_type=F32)
        o_ref[rows, :] = (ga_ref[rows, :].astype(F32) * y_a
                          + gb_ref[rows, :].astype(F32) * y_b).astype(o_ref.dtype)


def merge_branches(y_ssd, sc, w_a, w_b, p_sg, layer):
    bn = MERGE_BN
    ga0 = SG_GATE // bn
    gb0 = (SG_GATE + D_MODEL) // bn
    return pl.pallas_call(
        _merge_kernel,
        out_shape=jax.ShapeDtypeStruct((SEQ, D_MODEL), BF16),
        grid=(SEQ // BM, D_MODEL // bn),
        in_specs=[pl.BlockSpec((BM, D_SSD), lambda i, j: (i, 0)),
                  pl.BlockSpec((BM, D_SC), lambda i, j: (i, 0)),
                  pl.BlockSpec((None, D_SSD, bn), lambda i, j: (layer, 0, j)),
                  pl.BlockSpec((None, D_SC, bn), lambda i, j: (layer, 0, j)),
                  pl.BlockSpec((BM, bn), lambda i, j: (i, ga0 + j)),
                  pl.BlockSpec((BM, bn), lambda i, j: (i, gb0 + j))],
        out_specs=pl.BlockSpec((BM, bn), lambda i, j: (i, j)),
        compiler_params=_params("parallel", "parallel"),
        name="merge_branches",
    )(y_ssd, sc, w_a, w_b, p_sg, p_sg)


def _emit_residual(o, rows, g_ref, o_ref, xg_ref, ssq_ref):
    o_ref[rows, :] = o
    xg_ref[rows, :] = (o * g_ref[...]).astype(xg_ref.dtype)
    ssq_ref[rows, :] += _sum_squares(o, ssq_ref.at[rows, :])


def _mm_residual_kernel(a_ref, w_ref, x_ref, g_ref, o_ref, xg_ref, ssq_ref):
    @pl.when(pl.program_id(1) == 0)
    def _():
        ssq_ref[...] = jnp.zeros_like(ssq_ref)

    w = w_ref[...].astype(BF16)
    for rows in _row_groups(a_ref.shape[0]):
        o = x_ref[rows, :] + jnp.dot(a_ref[rows, :], w, preferred_element_type=F32)
        _emit_residual(o, rows, g_ref, o_ref, xg_ref, ssq_ref)


def _residual_outs(m, n, bm, bn, index):
    shapes = (jax.ShapeDtypeStruct((m, n), F32), jax.ShapeDtypeStruct((m, n), BF16),
              jax.ShapeDtypeStruct((m, LANES), F32))
    specs = (pl.BlockSpec((bm, bn), index), pl.BlockSpec((bm, bn), index),
             pl.BlockSpec((bm, LANES), lambda i, *_: (i, 0)))
    return shapes, specs


def matmul_residual(a, w, x, g_next, layer, bn=512):
    m, k = a.shape
    n = w.shape[2]
    shapes, specs = _residual_outs(m, n, BM, bn, lambda i, j: (i, j))
    return pl.pallas_call(
        _mm_residual_kernel,
        out_shape=shapes,
        grid=(m // BM, n // bn),
        in_specs=[pl.BlockSpec((BM, k), lambda i, j: (i, 0)),
                  pl.BlockSpec((None, k, bn), lambda i, j: (layer, 0, j)),
                  pl.BlockSpec((BM, bn), lambda i, j: (i, j)),
                  pl.BlockSpec((1, bn), lambda i, j: (0, j))],
        out_specs=specs,
        compiler_params=_params("parallel", "arbitrary"),
        name="matmul_residual",
    )(a, w, x, g_next.reshape(1, -1))


FF_BM = 2048


def _ffn_up_kernel(a_ref, ssq_ref, wg_ref, wu_ref, o_ref):
    @pl.when(pl.program_id(1) < FF_BLOCKS)
    def _():
        w_gate = wg_ref[...].astype(BF16)
        w_up = wu_ref[...].astype(BF16)
        for rows in _row_groups(FF_BM):
            a = a_ref[rows, :]
            scale = _row_scale(ssq_ref.at[rows, :], FF_BN)
            gate = jnp.dot(a, w_gate, preferred_element_type=F32) * scale
            up = jnp.dot(a, w_up, preferred_element_type=F32) * scale
            o_ref[rows, :] = (jax.nn.silu(gate) * up).astype(o_ref.dtype)

    @pl.when(pl.program_id(1) >= FF_BLOCKS)
    def _():
        o_ref[...] = jnp.zeros_like(o_ref)


def ffn_up(xg, ssq, w_gate, w_up, layer):
    m, k = xg.shape
    w_spec = pl.BlockSpec((None, k, FF_BN), lambda i, j: (layer, 0, jnp.minimum(j, FF_BLOCKS - 1)))
    return pl.pallas_call(
        _ffn_up_kernel,
        out_shape=jax.ShapeDtypeStruct((m, D_FF_PAD), BF16),
        grid=(m // FF_BM, D_FF_PAD // FF_BN),
        in_specs=[pl.BlockSpec((FF_BM, k), lambda i, j: (i, 0), pipeline_mode=pl.Buffered(1)),
                  pl.BlockSpec((FF_BM, LANES), lambda i, j: (i, 0)),
                  w_spec, w_spec],
        out_specs=pl.BlockSpec((FF_BM, FF_BN), lambda i, j: (i, j)),
        compiler_params=_params("parallel", "parallel"),
        name="ffn_up",
    )(xg, ssq, w_gate, w_up)


def _ffn_down_kernel(a_ref, w_ref, x_ref, g_ref, o_ref, xg_ref, ssq_ref, acc_ref):
    k = pl.program_id(2)
    last = pl.num_programs(2) - 1
    groups = _row_groups(a_ref.shape[0])
    partial = lambda rows: jnp.dot(a_ref[rows, :], w_ref[...], preferred_element_type=F32)

    @pl.when(k == 0)
    def _():
        for rows in groups:
            acc_ref[rows, :] = x_ref[rows, :] + partial(rows)

    @pl.when(jnp.logical_and(k > 0, k < last))
    def _():
        for rows in groups:
            acc_ref[rows, :] += partial(rows)

    @pl.when(jnp.logical_and(k == last, pl.program_id(1) == 0))
    def _():
        ssq_ref[...] = jnp.zeros_like(ssq_ref)

    @pl.when(k == last)
    def _():
        for rows in groups:
            _emit_residual(acc_ref[rows, :] + partial(rows), rows, g_ref, o_ref, xg_ref, ssq_ref)


def ffn_down(hid, w_down, x, g_next, layer):
    m = hid.shape[0]
    n = w_down.shape[2]
    tk = FFN_DOWN_TK
    shapes, specs = _residual_outs(m, n, BM, BN, lambda i, j, k: (i, j))
    return pl.pallas_call(
        _ffn_down_kernel,
        out_shape=shapes,
        grid=(m // BM, n // BN, D_FF_PAD // tk),
        in_specs=[pl.BlockSpec((BM, tk), lambda i, j, k: (i, k)),
                  pl.BlockSpec((None, tk, BN), lambda i, j, k: (layer, k, j)),
                  pl.BlockSpec((BM, BN), lambda i, j, k: (i, j)),
                  pl.BlockSpec((1, BN), lambda i, j, k: (0, j))],
        out_specs=specs,
        scratch_shapes=[pltpu.VMEM((BM, BN), F32)],
        compiler_params=_params("parallel", "arbitrary", "arbitrary"),
        name="ffn_down",
    )(hid, w_down, x, g_next.reshape(1, -1))


def _head_selection(offset):
    rows = jnp.arange(DT_COLS)[:, None]
    heads = jnp.arange(D_SSD)[None, :] // HEAD_DIM
    return (rows == heads + offset).astype(BF16)


def _pad_lanes(v, width):
    return jnp.pad(v, ((0, 0), (0, width - v.shape[1])))


def kernel(x, mix_norm_g, w_in, b_gate, ssd_conv_w, ssd_conv_b, dt_bias_f, dt_bias_b, a_log_f, a_log_b, d_skip, ssd_norm_g, w_ssd_out, sc_conv_w, w_sc_out, w_o, ffn_norm_g, w_ffn_gate, w_ffn_up, w_ffn_down, final_norm_g):
    w_sg = w_in[:, :, W_IN_SC:].astype(BF16)
    w_down = jnp.pad(w_ffn_down, ((0, 0), (0, D_FF_PAD - D_FF), (0, 0))).astype(BF16)
    dt_bias = _pad_lanes(jnp.concatenate([dt_bias_f, dt_bias_b], axis=1), DT_COLS)
    a_log = _pad_lanes(jnp.concatenate([a_log_f, a_log_b], axis=1), DT_COLS)
    d_skip_x = jnp.repeat(d_skip, HEAD_DIM, axis=1)
    e_sel_f = _head_selection(0)
    e_sel_b = _head_selection(HEADS)

    h = x.reshape(SEQ, D_MODEL)
    xg, ssq = norm_prep(h, mix_norm_g[0])
    for l in range(DEPTH):
        p_zx = inproj_cols(xg, ssq, w_in, l, 0, ZX_COLS, 512, BF16, "inproj_zx")
        dt_raw = inproj_cols(xg, ssq, w_in, l, W_IN_DT // DT_COLS, DT_COLS, DT_COLS, F32, "inproj_dt")
        p_sg = inproj_sg(xg, ssq, w_sg, b_gate[l], l)
        xbc = ssd_conv(p_zx, ssd_conv_w[l], ssd_conv_b[l])
        sc = short_conv(p_sg, sc_conv_w[l])
        y_ssd = ssd_scan(xbc, dt_raw, p_zx, dt_bias[l:l + 1], a_log[l:l + 1], e_sel_f, e_sel_b,
                         d_skip_x[l:l + 1], ssd_norm_g[l:l + 1])
        merged = merge_branches(y_ssd, sc, w_ssd_out, w_sc_out, p_sg, l)
        h, xg, ssq = matmul_residual(merged, w_o, h, ffn_norm_g[l], l)
        hid = ffn_up(xg, ssq, w_ffn_gate, w_ffn_up, l)
        g_next = mix_norm_g[l + 1] if l + 1 < DEPTH else final_norm_g
        h, xg, ssq = ffn_down(hid, w_down, h, g_next, l)
    out = rmsnorm(h, final_norm_g, F32)
    return out.reshape(x.shape)
```

```python
import functools

import jax
import jax.numpy as jnp
from jax import lax
from jax.experimental import pallas as pl
from jax.experimental.pallas import tpu as pltpu

F32 = jnp.float32
BF16 = jnp.bfloat16

D_MODEL = 4096
SEQ = 8192
DEPTH = 4
HEAD_DIM = 64
D_SSD = D_MODEL // 2
HEADS = D_SSD // HEAD_DIM
GROUPS = 4
HEADS_PER_GROUP = HEADS // GROUPS
STATE = 128
GROUP_WIDTH = D_SSD // GROUPS
SSD_CONV = 5
CHUNK = 128
N_CHUNKS = SEQ // CHUNK
XBC_DIM = D_SSD + 2 * GROUPS * STATE
D_SC = D_MODEL // 4
SC_CONV = 3
D_FF = 11008
EPS = 1e-6
LOG2_E = 1.4426950408889634

LANES = 128
BF16_SUBLANES = 16
VMEM_LIMIT_BYTES = 56 * 1024 * 1024

W_IN_DT = D_SSD + XBC_DIM
W_IN_SC = W_IN_DT + 2 * HEADS
DT_COLS = LANES
ZX_COLS = D_SSD + XBC_DIM
SG_GATE = 3 * D_SC
SG_COLS = SG_GATE + 2 * D_MODEL

FF_BN = 256
FF_BLOCKS = D_FF // FF_BN
D_FF_PAD = 11264
FFN_DOWN_TK = D_FF_PAD // 4

BM = 1024
BN = 1024
MM_ROWS = 256


def _params(*sem):
    return pltpu.CompilerParams(dimension_semantics=sem, vmem_limit_bytes=VMEM_LIMIT_BYTES)


def _row_scale(ssq_ref, width):
    r = lax.rsqrt(ssq_ref[...] * (1.0 / D_MODEL) + EPS)
    return jnp.concatenate([r] * (width // LANES), axis=1) if width > LANES else r


def _sum_squares(o, like_ref):
    return jnp.broadcast_to(jnp.sum(o * o, axis=-1, keepdims=True), like_ref.shape)


def _norm_prep_kernel(x_ref, g_ref, xg_ref, ssq_ref):
    x = x_ref[...]
    xg_ref[...] = (x * g_ref[...]).astype(xg_ref.dtype)
    ssq_ref[...] = _sum_squares(x, ssq_ref)


def norm_prep(x, g, rows=256):
    m, d = x.shape
    return pl.pallas_call(
        _norm_prep_kernel,
        out_shape=(jax.ShapeDtypeStruct((m, d), BF16), jax.ShapeDtypeStruct((m, LANES), F32)),
        grid=(m // rows,),
        in_specs=[pl.BlockSpec((rows, d), lambda i: (i, 0)),
                  pl.BlockSpec((1, d), lambda i: (0, 0))],
        out_specs=(pl.BlockSpec((rows, d), lambda i: (i, 0)),
                   pl.BlockSpec((rows, LANES), lambda i: (i, 0))),
        compiler_params=_params("parallel"),
        name="norm_prep",
    )(x, g.reshape(1, d))


def _rmsnorm_kernel(x_ref, g_ref, o_ref):
    x = x_ref[...]
    y = x * lax.rsqrt(jnp.mean(x * x, axis=-1, keepdims=True) + EPS)
    o_ref[...] = (y * g_ref[...]).astype(o_ref.dtype)


def rmsnorm(x, g, out_dtype, rows=256):
    m, d = x.shape
    return pl.pallas_call(
        _rmsnorm_kernel,
        out_shape=jax.ShapeDtypeStruct((m, d), out_dtype),
        grid=(m // rows,),
        in_specs=[pl.BlockSpec((rows, d), lambda i: (i, 0)),
                  pl.BlockSpec((1, d), lambda i: (0, 0))],
        out_specs=pl.BlockSpec((rows, d), lambda i: (i, 0)),
        compiler_params=_params("parallel"),
        name="rmsnorm",
    )(x, g.reshape(1, d))


def _row_groups(n_rows):
    return [slice(r, r + MM_ROWS) for r in range(0, n_rows, MM_ROWS)]


IN_BN = 512
_NT = (((1,), (1,)), ((), ()))


def _inproj_kernel(a_ref, ssq_ref, wt_ref, b_ref, o_ref, *, gate_block0):
    w_t = wt_ref[0].astype(BF16)
    for rows in _row_groups(a_ref.shape[0]):
        acc = lax.dot_general(a_ref[rows, :], w_t, _NT, preferred_element_type=F32)
        v = acc * _row_scale(ssq_ref.at[rows, :], o_ref.shape[1])
        if gate_block0 is not None:
            v = jnp.where(pl.program_id(1) >= gate_block0, jax.nn.sigmoid(v + b_ref[...]), v)
        o_ref[rows, :] = v.astype(o_ref.dtype)


def inproj(xg, ssq, w_in_t, b_gate, layer, col0, n_cols, bn, out_dtype, name, gate_col0=None):
    m, k = xg.shape
    gate_block0 = None if gate_col0 is None else (gate_col0 - col0) // bn
    gate_index = lambda j: 0 if gate_block0 is None else jnp.maximum(j - gate_block0, 0)
    return pl.pallas_call(
        functools.partial(_inproj_kernel, gate_block0=gate_block0),
        out_shape=jax.ShapeDtypeStruct((m, n_cols), out_dtype),
        grid=(m // BM, n_cols // bn),
        in_specs=[pl.BlockSpec((BM, k), lambda i, j: (i, 0)),
                  pl.BlockSpec((BM, LANES), lambda i, j: (i, 0)),
                  pl.BlockSpec((pl.Element(1), pl.Element(bn), pl.Element(k)),
                               lambda i, j: (layer, pl.multiple_of(col0 + j * bn, 64), 0)),
                  pl.BlockSpec((1, bn), lambda i, j: (0, gate_index(j)))],
        out_specs=pl.BlockSpec((BM, bn), lambda i, j: (i, j)),
        compiler_params=_params("parallel", "parallel"),
        name=name,
    )(xg, ssq, w_in_t, b_gate.reshape(1, -1))


CONV_ROWS = 512
CONV_COLS = 256
HALO = BF16_SUBLANES


def _with_halo(prev_ref, cur_ref, next_ref):
    i = pl.program_id(1)
    last = pl.num_programs(1) - 1
    prev = jnp.where(i == 0, 0.0, prev_ref[...].astype(F32))
    nxt = jnp.where(i == last, 0.0, next_ref[...].astype(F32))
    return jnp.concatenate([prev, cur_ref[...].astype(F32), nxt], axis=0)


def _centred_conv(u, w, width):
    pad = width // 2
    out = u[HALO - pad:HALO - pad + CONV_ROWS, :] * w[0:1, :]
    for t in range(1, width):
        out = out + u[HALO - pad + t:HALO - pad + t + CONV_ROWS, :] * w[t:t + 1, :]
    return out


def _halo_specs(col_block0):
    r = CONV_ROWS // HALO
    n_halo_blocks = SEQ // HALO
    return [
        pl.BlockSpec((HALO, CONV_COLS), lambda j, i: (jnp.maximum(i * r - 1, 0), col_block0 + j)),
        pl.BlockSpec((CONV_ROWS, CONV_COLS), lambda j, i: (i, col_block0 + j)),
        pl.BlockSpec((HALO, CONV_COLS),
                     lambda j, i: (jnp.minimum((i + 1) * r, n_halo_blocks - 1), col_block0 + j)),
    ]


def _ssd_conv_kernel(prev_ref, cur_ref, next_ref, w_ref, b_ref, o_ref):
    u = _with_halo(prev_ref, cur_ref, next_ref)
    o_ref[...] = jax.nn.silu(_centred_conv(u, w_ref[...], SSD_CONV) + b_ref[...]).astype(o_ref.dtype)


def ssd_conv(p_zx, conv_w, conv_b):
    return pl.pallas_call(
        _ssd_conv_kernel,
        out_shape=jax.ShapeDtypeStruct((SEQ, XBC_DIM), BF16),
        grid=(XBC_DIM // CONV_COLS, SEQ // CONV_ROWS),
        in_specs=_halo_specs(D_SSD // CONV_COLS) + [
            pl.BlockSpec((SSD_CONV, CONV_COLS), lambda j, i: (0, j)),
            pl.BlockSpec((1, CONV_COLS), lambda j, i: (0, j))],
        out_specs=pl.BlockSpec((CONV_ROWS, CONV_COLS), lambda j, i: (i, j)),
        compiler_params=_params("parallel", "parallel"),
        name="ssd_conv",
    )(p_zx, p_zx, p_zx, conv_w, conv_b.reshape(1, -1))


def _short_conv_kernel(b_ref, cprev_ref, c_ref, cnext_ref, hprev_ref, h_ref, hnext_ref, w_ref, o_ref):
    u = _with_halo(cprev_ref, c_ref, cnext_ref) * _with_halo(hprev_ref, h_ref, hnext_ref)
    o_ref[...] = (b_ref[...].astype(F32) * _centred_conv(u, w_ref[...], SC_CONV)).astype(o_ref.dtype)


def short_conv(p_sg, conv_w):
    c0 = D_SC // CONV_COLS
    h0 = 2 * D_SC // CONV_COLS
    return pl.pallas_call(
        _short_conv_kernel,
        out_shape=jax.ShapeDtypeStruct((SEQ, D_SC), BF16),
        grid=(D_SC // CONV_COLS, SEQ // CONV_ROWS),
        in_specs=[pl.BlockSpec((CONV_ROWS, CONV_COLS), lambda j, i: (i, j))]
        + _halo_specs(c0) + _halo_specs(h0)
        + [pl.BlockSpec((SC_CONV, CONV_COLS), lambda j, i: (0, j))],
        out_specs=pl.BlockSpec((CONV_ROWS, CONV_COLS), lambda j, i: (i, j)),
        compiler_params=_params("parallel", "parallel"),
        name="short_conv",
    )(p_sg, p_sg, p_sg, p_sg, p_sg, p_sg, p_sg, conv_w)


def _expand_heads(v, e_hi):
    hi = v.astype(BF16)
    lo = (v - hi.astype(F32)).astype(BF16)
    return (jnp.dot(hi, e_hi, preferred_element_type=F32)
            + jnp.dot(lo, e_hi, preferred_element_type=F32))


PREP_CHUNKS = 8


def _ssd_prep_kernel(dt_ref, dtb_ref, alog_ref, g2_ref, srct_ref, w_ref, dout_ref):
    row = lax.broadcasted_iota(jnp.int32, (CHUNK, CHUNK), 0)
    col = lax.broadcasted_iota(jnp.int32, (CHUNK, CHUNK), 1)
    lower = (row >= col).astype(F32)
    upper = (row <= col).astype(F32)
    is_fwd = lax.broadcasted_iota(jnp.int32, (CHUNK, DT_COLS), 1) < HEADS
    a_neg = -jnp.exp(alog_ref[...])
    for c in range(PREP_CHUNKS):
        rows = slice(c * CHUNK, (c + 1) * CHUNK)
        dt = jax.nn.softplus(dt_ref[rows, :] + dtb_ref[...])
        a = dt * a_neg
        hp = lax.Precision.HIGHEST
        g = (jnp.dot(lower, jnp.where(is_fwd, a, 0.0), preferred_element_type=F32, precision=hp)
             + jnp.dot(upper, jnp.where(is_fwd, 0.0, a), preferred_element_type=F32, precision=hp))
        g2 = LOG2_E * g
        g2_tot = jnp.where(is_fwd[0:1, :], g2[CHUNK - 1:CHUNK, :], g2[0:1, :])
        g2_ref[rows, :] = g2
        srct_ref[rows, :] = (g2 - jnp.log2(dt)).T
        w_ref[rows, :] = dt * jnp.exp2(g2_tot - g2)
        dout_ref[c * 8:(c + 1) * 8, :] = jnp.broadcast_to(jnp.exp2(g2_tot), (8, DT_COLS))


def ssd_prep(dt_raw, dt_bias, a_log):
    rows = PREP_CHUNKS * CHUNK
    full = jax.ShapeDtypeStruct((SEQ, DT_COLS), F32)
    blk = pl.BlockSpec((rows, DT_COLS), lambda i: (i, 0))
    vec = pl.BlockSpec((1, DT_COLS), lambda i: (0, 0))
    return pl.pallas_call(
        _ssd_prep_kernel,
        out_shape=(full, full, full, jax.ShapeDtypeStruct((N_CHUNKS * 8, DT_COLS), F32)),
        grid=(SEQ // rows,),
        in_specs=[blk, vec, vec],
        out_specs=(blk, blk, blk, pl.BlockSpec((PREP_CHUNKS * 8, DT_COLS), lambda i: (i, 0))),
        compiler_params=_params("parallel"),
        name="ssd_prep",
    )(dt_raw, dt_bias, a_log)


def _ssd_chunk(xs_ref, b_ref, c_ref, g2_ref, srct_ref, w_ref, dout_ref, e_ref, state_ref, *, reverse):
    off = HEADS if reverse else 0
    row = lax.broadcasted_iota(jnp.int32, (CHUNK, CHUNK), 0)
    col = lax.broadcasted_iota(jnp.int32, (CHUNK, CHUNK), 1)
    mask = (row <= col) if reverse else (row >= col)
    g2 = g2_ref[...]
    src_t = srct_ref[...]

    xs = xs_ref[...]
    state = state_ref[...]
    state_bf = state.astype(BF16)
    lane_in_tile = lax.broadcasted_iota(jnp.int32, (CHUNK + STATE, LANES), 1)

    ys = []
    for grp in range(GROUPS):
        b_g = b_ref[:, grp * STATE:(grp + 1) * STATE]
        c_g = c_ref[:, grp * STATE:(grp + 1) * STATE]
        cb = lax.dot_general(c_g, b_g, (((1,), (1,)), ((), ())), preferred_element_type=F32)
        c_f = c_g.astype(F32)
        for r in range(HEADS_PER_GROUP):
            h = grp * HEADS_PER_GROUP + r
            g_col = jnp.broadcast_to(g2[:, off + h:off + h + 1], (CHUNK, CHUNK))
            lmat_dt = jnp.where(mask, jnp.exp2(g_col - src_t[off + h:off + h + 1, :]), 0.0)
            m_h = (cb * lmat_dt).astype(BF16)
            c_s = (c_f * jnp.exp2(g_col)).astype(BF16)
            lhs = jnp.concatenate([m_h, c_s], axis=1)
            pair = slice((h // 2) * LANES, (h // 2 + 1) * LANES)
            own = (lane_in_tile >= HEAD_DIM) if h % 2 else (lane_in_tile < HEAD_DIM)
            rhs = jnp.concatenate([xs[:, pair], state_bf[:, pair]], axis=0)
            part = jnp.dot(lhs, jnp.where(own, rhs, jnp.zeros_like(rhs)), preferred_element_type=F32)
            if h % 2:
                ys.append(ys.pop() + part)
            else:
                ys.append(part)
    y = jnp.concatenate(ys, axis=1)

    e_hi = e_ref[...]
    xw = (xs.astype(F32) * _expand_heads(w_ref[...], e_hi)).astype(BF16)
    decay = _expand_heads(dout_ref[...], e_hi)[0:1, :]
    new = []
    for grp in range(GROUPS):
        b_g = b_ref[:, grp * STATE:(grp + 1) * STATE]
        new.append(lax.dot_general(b_g, xw[:, grp * GROUP_WIDTH:(grp + 1) * GROUP_WIDTH],
                                   (((0,), (0,)), ((), ())), preferred_element_type=F32))
    state_ref[...] = state * decay + jnp.concatenate(new, axis=1)
    return y


def _ssd_fwd_kernel(*refs):
    *chunk_refs, y_ref, state_ref = refs

    @pl.when(pl.program_id(0) == 0)
    def _():
        state_ref[...] = jnp.zeros_like(state_ref)

    y_ref[...] = _ssd_chunk(*chunk_refs, state_ref, reverse=False)


def _ssd_bwd_kernel(*refs):
    *chunk_refs, yf_ref, z_ref, dskip_ref, ng_ref, o_ref, state_ref = refs
    xs_ref = chunk_refs[0]

    @pl.when(pl.program_id(0) == 0)
    def _():
        state_ref[...] = jnp.zeros_like(state_ref)

    y = _ssd_chunk(*chunk_refs, state_ref, reverse=True)
    y = y + yf_ref[...] + dskip_ref[...] * xs_ref[...].astype(F32)
    y = y * jax.nn.silu(z_ref[...].astype(F32))
    ng = ng_ref[...]
    outs = []
    for grp in range(GROUPS):
        sl = slice(grp * GROUP_WIDTH, (grp + 1) * GROUP_WIDTH)
        yg = y[:, sl]
        yg = yg * lax.rsqrt(jnp.mean(yg * yg, axis=-1, keepdims=True) + EPS)
        outs.append(yg * ng[:, sl])
    o_ref[...] = jnp.concatenate(outs, axis=1).astype(o_ref.dtype)


def ssd_scan(xbc, dt_raw, p_zx, dt_bias, a_log, e_sel_f, e_sel_b, d_skip_x, norm_g):
    gn = GROUPS * STATE
    b_blk = D_SSD // gn
    decays = ssd_prep(dt_raw, dt_bias, a_log)
    per_chunk = lambda cm: pl.BlockSpec((CHUNK, DT_COLS), lambda c: (cm(c), 0))
    common = lambda cm: [
        pl.BlockSpec((CHUNK, D_SSD), lambda c: (cm(c), 0)),
        pl.BlockSpec((CHUNK, gn), lambda c: (cm(c), b_blk)),
        pl.BlockSpec((CHUNK, gn), lambda c: (cm(c), b_blk + 1)),
        per_chunk(cm), per_chunk(cm), per_chunk(cm),
        pl.BlockSpec((8, DT_COLS), lambda c: (cm(c), 0)),
        pl.BlockSpec((DT_COLS, D_SSD), lambda c: (0, 0)),
    ]
    fwd = lambda c: c
    bwd = lambda c: N_CHUNKS - 1 - c
    y_f = pl.pallas_call(
        _ssd_fwd_kernel,
        out_shape=jax.ShapeDtypeStruct((SEQ, D_SSD), F32),
        grid=(N_CHUNKS,),
        in_specs=common(fwd),
        out_specs=pl.BlockSpec((CHUNK, D_SSD), lambda c: (c, 0)),
        scratch_shapes=[pltpu.VMEM((STATE, D_SSD), F32)],
        compiler_params=_params("arbitrary"),
        name="ssd_fwd",
    )(xbc, xbc, xbc, *decays, e_sel_f)
    return pl.pallas_call(
        _ssd_bwd_kernel,
        out_shape=jax.ShapeDtypeStruct((SEQ, D_SSD), BF16),
        grid=(N_CHUNKS,),
        in_specs=common(bwd) + [
            pl.BlockSpec((CHUNK, D_SSD), lambda c: (bwd(c), 0)),
            pl.BlockSpec((CHUNK, D_SSD), lambda c: (bwd(c), 0)),
            pl.BlockSpec((1, D_SSD), lambda c: (0, 0)),
            pl.BlockSpec((1, D_SSD), lambda c: (0, 0)),
        ],
        out_specs=pl.BlockSpec((CHUNK, D_SSD), lambda c: (bwd(c), 0)),
        scratch_shapes=[pltpu.VMEM((STATE, D_SSD), F32)],
        compiler_params=_params("arbitrary"),
        name="ssd_bwd",
    )(xbc, xbc, xbc, *decays, e_sel_b, y_f, p_zx, d_skip_x, norm_g)


MERGE_BN = 512
MERGE_BM = 2048


def _merge_kernel(ya_ref, sc_ref, wa_ref, wb_ref, ga_ref, gb_ref, o_ref):
    w_a = wa_ref[...].astype(BF16)
    w_b = wb_ref[...].astype(BF16)
    for rows in _row_groups(ya_ref.shape[0]):
        y_a = jnp.dot(ya_ref[rows, :], w_a, preferred_element_type=F32)
        y_b = jnp.dot(sc_ref[rows, :], w_b, preferred_element_type=F32)
        o_ref[rows, :] = (ga_ref[rows, :].astype(F32) * y_a
                          + gb_ref[rows, :].astype(F32) * y_b).astype(o_ref.dtype)


def merge_branches(y_ssd, sc, w_a, w_b, p_sg, layer):
    bn = MERGE_BN
    bm = MERGE_BM
    ga0 = SG_GATE // bn
    gb0 = (SG_GATE + D_MODEL) // bn
    resident = pl.Buffered(1)
    return pl.pallas_call(
        _merge_kernel,
        out_shape=jax.ShapeDtypeStruct((SEQ, D_MODEL), BF16),
        grid=(SEQ // bm, D_MODEL // bn),
        in_specs=[pl.BlockSpec((bm, D_SSD), lambda i, j: (i, 0), pipeline_mode=resident),
                  pl.BlockSpec((bm, D_SC), lambda i, j: (i, 0), pipeline_mode=resident),
                  pl.BlockSpec((None, D_SSD, bn), lambda i, j: (layer, 0, j)),
                  pl.BlockSpec((None, D_SC, bn), lambda i, j: (layer, 0, j)),
                  pl.BlockSpec((bm, bn), lambda i, j: (i, ga0 + j)),
                  pl.BlockSpec((bm, bn), lambda i, j: (i, gb0 + j))],
        out_specs=pl.BlockSpec((bm, bn), lambda i, j: (i, j)),
        compiler_params=_params("parallel", "parallel"),
        name="merge_branches",
    )(y_ssd, sc, w_a, w_b, p_sg, p_sg)


def _emit_residual(o, rows, g_ref, o_ref, xg_ref, ssq_ref):
    o_ref[rows, :] = o
    xg_ref[rows, :] = (o * g_ref[...]).astype(xg_ref.dtype)
    ssq_ref[rows, :] += _sum_squares(o, ssq_ref.at[rows, :])


def _mm_residual_kernel(a_ref, w_ref, x_ref, g_ref, o_ref, xg_ref, ssq_ref):
    @pl.when(pl.program_id(1) == 0)
    def _():
        ssq_ref[...] = jnp.zeros_like(ssq_ref)

    w = w_ref[...].astype(BF16)
    for rows in _row_groups(a_ref.shape[0]):
        o = x_ref[rows, :] + jnp.dot(a_ref[rows, :], w, preferred_element_type=F32)
        _emit_residual(o, rows, g_ref, o_ref, xg_ref, ssq_ref)


def _residual_outs(m, n, bm, bn, index):
    shapes = (jax.ShapeDtypeStruct((m, n), F32), jax.ShapeDtypeStruct((m, n), BF16),
              jax.ShapeDtypeStruct((m, LANES), F32))
    specs = (pl.BlockSpec((bm, bn), index), pl.BlockSpec((bm, bn), index),
             pl.BlockSpec((bm, LANES), lambda i, *_: (i, 0)))
    return shapes, specs


def matmul_residual(a, w, x, g_next, layer, bn=512):
    m, k = a.shape
    n = w.shape[2]
    shapes, specs = _residual_outs(m, n, BM, bn, lambda i, j: (i, j))
    return pl.pallas_call(
        _mm_residual_kernel,
        out_shape=shapes,
        grid=(m // BM, n // bn),
        in_specs=[pl.BlockSpec((BM, k), lambda i, j: (i, 0)),
                  pl.BlockSpec((None, k, bn), lambda i, j: (layer, 0, j)),
                  pl.BlockSpec((BM, bn), lambda i, j: (i, j)),
                  pl.BlockSpec((1, bn), lambda i, j: (0, j))],
        out_specs=specs,
        compiler_params=_params("parallel", "arbitrary"),
        name="matmul_residual",
    )(a, w, x, g_next.reshape(1, -1))


FF_BM = 2048


def _ffn_up_kernel(a_ref, ssq_ref, wg_ref, wu_ref, o_ref):
    @pl.when(pl.program_id(1) < FF_BLOCKS)
    def _():
        w_gate = wg_ref[...].astype(BF16)
        w_up = wu_ref[...].astype(BF16)
        for rows in _row_groups(FF_BM):
            a = a_ref[rows, :]
            scale = _row_scale(ssq_ref.at[rows, :], FF_BN)
            gate = jnp.dot(a, w_gate, preferred_element_type=F32) * scale
            up = jnp.dot(a, w_up, preferred_element_type=F32) * scale
            o_ref[rows, :] = (jax.nn.silu(gate) * up).astype(o_ref.dtype)

    @pl.when(pl.program_id(1) >= FF_BLOCKS)
    def _():
        o_ref[...] = jnp.zeros_like(o_ref)


def ffn_up(xg, ssq, w_gate, w_up, layer):
    m, k = xg.shape
    w_spec = pl.BlockSpec((None, k, FF_BN), lambda i, j: (layer, 0, jnp.minimum(j, FF_BLOCKS - 1)))
    return pl.pallas_call(
        _ffn_up_kernel,
        out_shape=jax.ShapeDtypeStruct((m, D_FF_PAD), BF16),
        grid=(m // FF_BM, D_FF_PAD // FF_BN),
        in_specs=[pl.BlockSpec((FF_BM, k), lambda i, j: (i, 0), pipeline_mode=pl.Buffered(1)),
                  pl.BlockSpec((FF_BM, LANES), lambda i, j: (i, 0)),
                  w_spec, w_spec],
        out_specs=pl.BlockSpec((FF_BM, FF_BN), lambda i, j: (i, j)),
        compiler_params=_params("parallel", "parallel"),
        name="ffn_up",
    )(xg, ssq, w_gate, w_up)


def _cast_pad_kernel(w_ref, o_ref):
    @pl.when(pl.program_id(1) < FF_BLOCKS)
    def _():
        o_ref[...] = w_ref[...].astype(o_ref.dtype)

    @pl.when(pl.program_id(1) >= FF_BLOCKS)
    def _():
        o_ref[...] = jnp.zeros_like(o_ref)


def cast_pad_rows(w_down):
    n = w_down.shape[2]
    return pl.pallas_call(
        _cast_pad_kernel,
        out_shape=jax.ShapeDtypeStruct((DEPTH, D_FF_PAD, n), BF16),
        grid=(DEPTH, D_FF_PAD // FF_BN),
        in_specs=[pl.BlockSpec((None, FF_BN, n), lambda l, r: (l, jnp.minimum(r, FF_BLOCKS - 1), 0))],
        out_specs=pl.BlockSpec((None, FF_BN, n), lambda l, r: (l, r, 0)),
        compiler_params=_params("parallel", "parallel"),
        name="cast_pad_rows",
    )(w_down)


def _ffn_down_kernel(a_ref, w_ref, x_ref, g_ref, o_ref, xg_ref, ssq_ref, acc_ref):
    k = pl.program_id(2)
    last = pl.num_programs(2) - 1
    groups = _row_groups(a_ref.shape[0])
    partial = lambda rows: jnp.dot(a_ref[rows, :], w_ref[...], preferred_element_type=F32)

    @pl.when(k == 0)
    def _():
        for rows in groups:
            acc_ref[rows, :] = x_ref[rows, :] + partial(rows)

    @pl.when(jnp.logical_and(k > 0, k < last))
    def _():
        for rows in groups:
            acc_ref[rows, :] += partial(rows)

    @pl.when(jnp.logical_and(k == last, pl.program_id(1) == 0))
    def _():
        ssq_ref[...] = jnp.zeros_like(ssq_ref)

    @pl.when(k == last)
    def _():
        for rows in groups:
            _emit_residual(acc_ref[rows, :] + partial(rows), rows, g_ref, o_ref, xg_ref, ssq_ref)


def ffn_down(hid, w_down, x, g_next, layer):
    m = hid.shape[0]
    n = w_down.shape[2]
    tk = FFN_DOWN_TK
    shapes, specs = _residual_outs(m, n, BM, BN, lambda i, j, k: (i, j))
    return pl.pallas_call(
        _ffn_down_kernel,
        out_shape=shapes,
        grid=(m // BM, n // BN, D_FF_PAD // tk),
        in_specs=[pl.BlockSpec((BM, tk), lambda i, j, k: (i, k)),
                  pl.BlockSpec((None, tk, BN), lambda i, j, k: (layer, k, j)),
                  pl.BlockSpec((BM, BN), lambda i, j, k: (i, j)),
                  pl.BlockSpec((1, BN), lambda i, j, k: (0, j))],
        out_specs=specs,
        scratch_shapes=[pltpu.VMEM((BM, BN), F32)],
        compiler_params=_params("parallel", "arbitrary", "arbitrary"),
        name="ffn_down",
    )(hid, w_down, x, g_next.reshape(1, -1))


def _head_selection(offset):
    rows = jnp.arange(DT_COLS)[:, None]
    heads = jnp.arange(D_SSD)[None, :] // HEAD_DIM
    return (rows == heads + offset).astype(BF16)


def _pad_lanes(v, width):
    return jnp.pad(v, ((0, 0), (0, width - v.shape[1])))


def kernel(x, mix_norm_g, w_in, b_gate, ssd_conv_w, ssd_conv_b, dt_bias_f, dt_bias_b, a_log_f, a_log_b, d_skip, ssd_norm_g, w_ssd_out, sc_conv_w, w_sc_out, w_o, ffn_norm_g, w_ffn_gate, w_ffn_up, w_ffn_down, final_norm_g):
    w_in_t = jnp.swapaxes(w_in, 1, 2)
    w_down = cast_pad_rows(w_ffn_down)
    dt_bias = _pad_lanes(jnp.concatenate([dt_bias_f, dt_bias_b], axis=1), DT_COLS)
    a_log = _pad_lanes(jnp.concatenate([a_log_f, a_log_b], axis=1), DT_COLS)
    d_skip_x = jnp.repeat(d_skip, HEAD_DIM, axis=1)
    e_sel_f = _head_selection(0)
    e_sel_b = _head_selection(HEADS)

    h = x.reshape(SEQ, D_MODEL)
    xg, ssq = norm_prep(h, mix_norm_g[0])
    for l in range(DEPTH):
        p_zx = inproj(xg, ssq, w_in_t, b_gate[l], l, 0, ZX_COLS, IN_BN, BF16, "inproj_zx")
        dt_raw = inproj(xg, ssq, w_in_t, b_gate[l], l, W_IN_DT, DT_COLS, DT_COLS, F32, "inproj_dt")
        p_sg = inproj(xg, ssq, w_in_t, b_gate[l], l, W_IN_SC, SG_COLS, IN_BN, BF16, "inproj_sg",
                      gate_col0=W_IN_SC + SG_GATE)
        xbc = ssd_conv(p_zx, ssd_conv_w[l], ssd_conv_b[l])
        sc = short_conv(p_sg, sc_conv_w[l])
        y_ssd = ssd_scan(xbc, dt_raw, p_zx, dt_bias[l:l + 1], a_log[l:l + 1], e_sel_f, e_sel_b,
                         d_skip_x[l:l + 1], ssd_norm_g[l:l + 1])
        merged = merge_branches(y_ssd, sc, w_ssd_out, w_sc_out, p_sg, l)
        h, xg, ssq = matmul_residual(merged, w_o, h, ffn_norm_g[l], l)
        hid = ffn_up(xg, ssq, w_ffn_gate, w_ffn_up, l)
        g_next = mix_norm_g[l + 1] if l + 1 < DEPTH else final_norm_g
        h, xg, ssq = ffn_down(hid, w_down, h, g_next, l)
    out = rmsnorm(h, final_norm_g, F32)
    return out.reshape(x.shape)
```

```python
import functools

import jax
import jax.numpy as jnp
from jax import lax
from jax.experimental import pallas as pl
from jax.experimental.pallas import tpu as pltpu

F32 = jnp.float32
BF16 = jnp.bfloat16

D_MODEL = 4096
SEQ = 8192
DEPTH = 4
HEAD_DIM = 64
D_SSD = D_MODEL // 2
HEADS = D_SSD // HEAD_DIM
GROUPS = 4
HEADS_PER_GROUP = HEADS // GROUPS
STATE = 128
GROUP_WIDTH = D_SSD // GROUPS
SSD_CONV = 5
CHUNK = 128
N_CHUNKS = SEQ // CHUNK
XBC_DIM = D_SSD + 2 * GROUPS * STATE
D_SC = D_MODEL // 4
SC_CONV = 3
D_FF = 11008
EPS = 1e-6
LOG2_E = 1.4426950408889634

LANES = 128
BF16_SUBLANES = 16
VMEM_LIMIT_BYTES = 56 * 1024 * 1024

W_IN_DT = D_SSD + XBC_DIM
W_IN_SC = W_IN_DT + 2 * HEADS
DT_COLS = LANES
ZX_COLS = D_SSD + XBC_DIM
SG_GATE = 3 * D_SC
SG_COLS = SG_GATE + 2 * D_MODEL

FF_BN = 256
FF_BLOCKS = D_FF // FF_BN
D_FF_PAD = 11264
FFN_DOWN_TK = D_FF_PAD // 4

BM = 1024
BN = 1024
BM_WIDE = 2048
MM_ROWS = 256


def _params(*sem):
    return pltpu.CompilerParams(dimension_semantics=sem, vmem_limit_bytes=VMEM_LIMIT_BYTES)


def _row_scale(ssq_ref, width):
    r = lax.rsqrt(ssq_ref[...] * (1.0 / D_MODEL) + EPS)
    return jnp.concatenate([r] * (width // LANES), axis=1) if width > LANES else r


def _sum_squares(o, like_ref):
    return jnp.broadcast_to(jnp.sum(o * o, axis=-1, keepdims=True), like_ref.shape)


def _norm_prep_kernel(x_ref, g_ref, xg_ref, ssq_ref):
    x = x_ref[...]
    xg_ref[...] = (x * g_ref[...]).astype(xg_ref.dtype)
    ssq_ref[...] = _sum_squares(x, ssq_ref)


def norm_prep(x, g, rows=256):
    m, d = x.shape
    return pl.pallas_call(
        _norm_prep_kernel,
        out_shape=(jax.ShapeDtypeStruct((m, d), BF16), jax.ShapeDtypeStruct((m, LANES), F32)),
        grid=(m // rows,),
        in_specs=[pl.BlockSpec((rows, d), lambda i: (i, 0)),
                  pl.BlockSpec((1, d), lambda i: (0, 0))],
        out_specs=(pl.BlockSpec((rows, d), lambda i: (i, 0)),
                   pl.BlockSpec((rows, LANES), lambda i: (i, 0))),
        compiler_params=_params("parallel"),
        name="norm_prep",
    )(x, g.reshape(1, d))


def _rmsnorm_kernel(x_ref, g_ref, o_ref):
    x = x_ref[...]
    y = x * lax.rsqrt(jnp.mean(x * x, axis=-1, keepdims=True) + EPS)
    o_ref[...] = (y * g_ref[...]).astype(o_ref.dtype)


def rmsnorm(x, g, out_dtype, rows=256):
    m, d = x.shape
    return pl.pallas_call(
        _rmsnorm_kernel,
        out_shape=jax.ShapeDtypeStruct((m, d), out_dtype),
        grid=(m // rows,),
        in_specs=[pl.BlockSpec((rows, d), lambda i: (i, 0)),
                  pl.BlockSpec((1, d), lambda i: (0, 0))],
        out_specs=pl.BlockSpec((rows, d), lambda i: (i, 0)),
        compiler_params=_params("parallel"),
        name="rmsnorm",
    )(x, g.reshape(1, d))


def _row_groups(n_rows):
    return [slice(r, r + MM_ROWS) for r in range(0, n_rows, MM_ROWS)]


IN_BN = 512
_NT = (((1,), (1,)), ((), ()))


def _inproj_kernel(a_ref, ssq_ref, wt_ref, b_ref, o_ref, *, gate_block0):
    w_t = wt_ref[0].astype(BF16)
    for rows in _row_groups(a_ref.shape[0]):
        acc = lax.dot_general(a_ref[rows, :], w_t, _NT, preferred_element_type=F32)
        v = acc * _row_scale(ssq_ref.at[rows, :], o_ref.shape[1])
        if gate_block0 is not None:
            v = jnp.where(pl.program_id(1) >= gate_block0, jax.nn.sigmoid(v + b_ref[...]), v)
        o_ref[rows, :] = v.astype(o_ref.dtype)


def inproj(xg, ssq, w_in_t, b_gate, layer, col0, n_cols, bn, out_dtype, name, gate_col0=None):
    m, k = xg.shape
    gate_block0 = None if gate_col0 is None else (gate_col0 - col0) // bn
    gate_index = lambda j: 0 if gate_block0 is None else jnp.maximum(j - gate_block0, 0)
    return pl.pallas_call(
        functools.partial(_inproj_kernel, gate_block0=gate_block0),
        out_shape=jax.ShapeDtypeStruct((m, n_cols), out_dtype),
        grid=(m // BM_WIDE, n_cols // bn),
        in_specs=[pl.BlockSpec((BM_WIDE, k), lambda i, j: (i, 0), pipeline_mode=pl.Buffered(1)),
                  pl.BlockSpec((BM_WIDE, LANES), lambda i, j: (i, 0)),
                  pl.BlockSpec((pl.Element(1), pl.Element(bn), pl.Element(k)),
                               lambda i, j: (layer, pl.multiple_of(col0 + j * bn, 64), 0)),
                  pl.BlockSpec((1, bn), lambda i, j: (0, gate_index(j)))],
        out_specs=pl.BlockSpec((BM_WIDE, bn), lambda i, j: (i, j)),
        compiler_params=_params("parallel", "parallel"),
        name=name,
    )(xg, ssq, w_in_t, b_gate.reshape(1, -1))


CONV_ROWS = 1024
CONV_COLS = 256
HALO = BF16_SUBLANES


def _with_halo(prev_ref, cur_ref, next_ref):
    i = pl.program_id(1)
    last = pl.num_programs(1) - 1
    prev = jnp.where(i == 0, 0.0, prev_ref[...].astype(F32))
    nxt = jnp.where(i == last, 0.0, next_ref[...].astype(F32))
    return jnp.concatenate([prev, cur_ref[...].astype(F32), nxt], axis=0)


def _centred_conv(u, w, width):
    pad = width // 2
    out = u[HALO - pad:HALO - pad + CONV_ROWS, :] * w[0:1, :]
    for t in range(1, width):
        out = out + u[HALO - pad + t:HALO - pad + t + CONV_ROWS, :] * w[t:t + 1, :]
    return out


def _halo_specs(col_block0):
    r = CONV_ROWS // HALO
    n_halo_blocks = SEQ // HALO
    return [
        pl.BlockSpec((HALO, CONV_COLS), lambda j, i: (jnp.maximum(i * r - 1, 0), col_block0 + j)),
        pl.BlockSpec((CONV_ROWS, CONV_COLS), lambda j, i: (i, col_block0 + j)),
        pl.BlockSpec((HALO, CONV_COLS),
                     lambda j, i: (jnp.minimum((i + 1) * r, n_halo_blocks - 1), col_block0 + j)),
    ]


def _ssd_conv_kernel(prev_ref, cur_ref, next_ref, w_ref, b_ref, o_ref):
    u = _with_halo(prev_ref, cur_ref, next_ref)
    o_ref[...] = jax.nn.silu(_centred_conv(u, w_ref[...], SSD_CONV) + b_ref[...]).astype(o_ref.dtype)


def ssd_conv(p_zx, conv_w, conv_b):
    return pl.pallas_call(
        _ssd_conv_kernel,
        out_shape=jax.ShapeDtypeStruct((SEQ, XBC_DIM), BF16),
        grid=(XBC_DIM // CONV_COLS, SEQ // CONV_ROWS),
        in_specs=_halo_specs(D_SSD // CONV_COLS) + [
            pl.BlockSpec((SSD_CONV, CONV_COLS), lambda j, i: (0, j)),
            pl.BlockSpec((1, CONV_COLS), lambda j, i: (0, j))],
        out_specs=pl.BlockSpec((CONV_ROWS, CONV_COLS), lambda j, i: (i, j)),
        compiler_params=_params("parallel", "parallel"),
        name="ssd_conv",
    )(p_zx, p_zx, p_zx, conv_w, conv_b.reshape(1, -1))


def _short_conv_kernel(b_ref, cprev_ref, c_ref, cnext_ref, hprev_ref, h_ref, hnext_ref, w_ref, o_ref):
    u = _with_halo(cprev_ref, c_ref, cnext_ref) * _with_halo(hprev_ref, h_ref, hnext_ref)
    o_ref[...] = (b_ref[...].astype(F32) * _centred_conv(u, w_ref[...], SC_CONV)).astype(o_ref.dtype)


def short_conv(p_sg, conv_w):
    c0 = D_SC // CONV_COLS
    h0 = 2 * D_SC // CONV_COLS
    return pl.pallas_call(
        _short_conv_kernel,
        out_shape=jax.ShapeDtypeStruct((SEQ, D_SC), BF16),
        grid=(D_SC // CONV_COLS, SEQ // CONV_ROWS),
        in_specs=[pl.BlockSpec((CONV_ROWS, CONV_COLS), lambda j, i: (i, j))]
        + _halo_specs(c0) + _halo_specs(h0)
        + [pl.BlockSpec((SC_CONV, CONV_COLS), lambda j, i: (0, j))],
        out_specs=pl.BlockSpec((CONV_ROWS, CONV_COLS), lambda j, i: (i, j)),
        compiler_params=_params("parallel", "parallel"),
        name="short_conv",
    )(p_sg, p_sg, p_sg, p_sg, p_sg, p_sg, p_sg, conv_w)


def _expand_heads(v, e_hi):
    hi = v.astype(BF16)
    lo = (v - hi.astype(F32)).astype(BF16)
    return (jnp.dot(hi, e_hi, preferred_element_type=F32)
            + jnp.dot(lo, e_hi, preferred_element_type=F32))


PREP_CHUNKS = 8


def _ssd_prep_kernel(dt_ref, dtb_ref, alog_ref, g2_ref, srct_ref, w_ref, dout_ref):
    row = lax.broadcasted_iota(jnp.int32, (CHUNK, CHUNK), 0)
    col = lax.broadcasted_iota(jnp.int32, (CHUNK, CHUNK), 1)
    lower = (row >= col).astype(F32)
    upper = (row <= col).astype(F32)
    is_fwd = lax.broadcasted_iota(jnp.int32, (CHUNK, DT_COLS), 1) < HEADS
    a_neg = -jnp.exp(alog_ref[...])
    for c in range(PREP_CHUNKS):
        rows = slice(c * CHUNK, (c + 1) * CHUNK)
        dt = jax.nn.softplus(dt_ref[rows, :] + dtb_ref[...])
        a = dt * a_neg
        hp = lax.Precision.HIGHEST
        g = (jnp.dot(lower, jnp.where(is_fwd, a, 0.0), preferred_element_type=F32, precision=hp)
             + jnp.dot(upper, jnp.where(is_fwd, 0.0, a), preferred_element_type=F32, precision=hp))
        g2 = LOG2_E * g
        g2_tot = jnp.where(is_fwd[0:1, :], g2[CHUNK - 1:CHUNK, :], g2[0:1, :])
        g2_ref[rows, :] = g2
        srct_ref[rows, :] = (g2 - jnp.log2(dt)).T
        w_ref[rows, :] = dt * jnp.exp2(g2_tot - g2)
        dout_ref[c * 8:(c + 1) * 8, :] = jnp.broadcast_to(jnp.exp2(g2_tot), (8, DT_COLS))


def ssd_prep(dt_raw, dt_bias, a_log):
    rows = PREP_CHUNKS * CHUNK
    full = jax.ShapeDtypeStruct((SEQ, DT_COLS), F32)
    blk = pl.BlockSpec((rows, DT_COLS), lambda i: (i, 0))
    vec = pl.BlockSpec((1, DT_COLS), lambda i: (0, 0))
    return pl.pallas_call(
        _ssd_prep_kernel,
        out_shape=(full, full, full, jax.ShapeDtypeStruct((N_CHUNKS * 8, DT_COLS), F32)),
        grid=(SEQ // rows,),
        in_specs=[blk, vec, vec],
        out_specs=(blk, blk, blk, pl.BlockSpec((PREP_CHUNKS * 8, DT_COLS), lambda i: (i, 0))),
        compiler_params=_params("parallel"),
        name="ssd_prep",
    )(dt_raw, dt_bias, a_log)


def _ssd_chunk(xs_ref, b_ref, c_ref, g2_ref, srct_ref, w_ref, dout_ref, e_ref, state_ref, *, reverse):
    off = HEADS if reverse else 0
    row = lax.broadcasted_iota(jnp.int32, (CHUNK, CHUNK), 0)
    col = lax.broadcasted_iota(jnp.int32, (CHUNK, CHUNK), 1)
    mask = (row <= col) if reverse else (row >= col)
    g2 = g2_ref[...]
    src_t = srct_ref[...]

    xs = xs_ref[...]
    state = state_ref[...]
    state_bf = state.astype(BF16)
    lane_in_tile = lax.broadcasted_iota(jnp.int32, (CHUNK + STATE, LANES), 1)

    ys = []
    for grp in range(GROUPS):
        b_g = b_ref[:, grp * STATE:(grp + 1) * STATE]
        c_g = c_ref[:, grp * STATE:(grp + 1) * STATE]
        cb = lax.dot_general(c_g, b_g, (((1,), (1,)), ((), ())), preferred_element_type=F32)
        c_f = c_g.astype(F32)
        for r in range(HEADS_PER_GROUP):
            h = grp * HEADS_PER_GROUP + r
            g_col = jnp.broadcast_to(g2[:, off + h:off + h + 1], (CHUNK, CHUNK))
            lmat_dt = jnp.where(mask, jnp.exp2(g_col - src_t[off + h:off + h + 1, :]), 0.0)
            m_h = (cb * lmat_dt).astype(BF16)
            c_s = (c_f * jnp.exp2(g_col)).astype(BF16)
            lhs = jnp.concatenate([m_h, c_s], axis=1)
            pair = slice((h // 2) * LANES, (h // 2 + 1) * LANES)
            own = (lane_in_tile >= HEAD_DIM) if h % 2 else (lane_in_tile < HEAD_DIM)
            rhs = jnp.concatenate([xs[:, pair], state_bf[:, pair]], axis=0)
            part = jnp.dot(lhs, jnp.where(own, rhs, jnp.zeros_like(rhs)), preferred_element_type=F32)
            if h % 2:
                ys.append(ys.pop() + part)
            else:
                ys.append(part)
    y = jnp.concatenate(ys, axis=1)

    e_hi = e_ref[...]
    xw = (xs.astype(F32) * _expand_heads(w_ref[...], e_hi)).astype(BF16)
    decay = _expand_heads(dout_ref[...], e_hi)[0:1, :]
    new = []
    for grp in range(GROUPS):
        b_g = b_ref[:, grp * STATE:(grp + 1) * STATE]
        new.append(lax.dot_general(b_g, xw[:, grp * GROUP_WIDTH:(grp + 1) * GROUP_WIDTH],
                                   (((0,), (0,)), ((), ())), preferred_element_type=F32))
    state_ref[...] = state * decay + jnp.concatenate(new, axis=1)
    return y


def _ssd_fwd_kernel(*refs):
    *chunk_refs, y_ref, state_ref = refs

    @pl.when(pl.program_id(0) == 0)
    def _():
        state_ref[...] = jnp.zeros_like(state_ref)

    y_ref[...] = _ssd_chunk(*chunk_refs, state_ref, reverse=False)


def _ssd_bwd_kernel(*refs):
    *chunk_refs, yf_ref, z_ref, dskip_ref, ng_ref, o_ref, state_ref = refs
    xs_ref = chunk_refs[0]

    @pl.when(pl.program_id(0) == 0)
    def _():
        state_ref[...] = jnp.zeros_like(state_ref)

    y = _ssd_chunk(*chunk_refs, state_ref, reverse=True)
    y = y + yf_ref[...] + dskip_ref[...] * xs_ref[...].astype(F32)
    y = y * jax.nn.silu(z_ref[...].astype(F32))
    ng = ng_ref[...]
    outs = []
    for grp in range(GROUPS):
        sl = slice(grp * GROUP_WIDTH, (grp + 1) * GROUP_WIDTH)
        yg = y[:, sl]
        yg = yg * lax.rsqrt(jnp.mean(yg * yg, axis=-1, keepdims=True) + EPS)
        outs.append(yg * ng[:, sl])
    o_ref[...] = jnp.concatenate(outs, axis=1).astype(o_ref.dtype)


def ssd_scan(xbc, dt_raw, p_zx, dt_bias, a_log, e_sel_f, e_sel_b, d_skip_x, norm_g):
    gn = GROUPS * STATE
    b_blk = D_SSD // gn
    decays = ssd_prep(dt_raw, dt_bias, a_log)
    per_chunk = lambda cm: pl.BlockSpec((CHUNK, DT_COLS), lambda c: (cm(c), 0))
    common = lambda cm: [
        pl.BlockSpec((CHUNK, D_SSD), lambda c: (cm(c), 0)),
        pl.BlockSpec((CHUNK, gn), lambda c: (cm(c), b_blk)),
        pl.BlockSpec((CHUNK, gn), lambda c: (cm(c), b_blk + 1)),
        per_chunk(cm), per_chunk(cm), per_chunk(cm),
        pl.BlockSpec((8, DT_COLS), lambda c: (cm(c), 0)),
        pl.BlockSpec((DT_COLS, D_SSD), lambda c: (0, 0)),
    ]
    fwd = lambda c: c
    bwd = lambda c: N_CHUNKS - 1 - c
    y_f = pl.pallas_call(
        _ssd_fwd_kernel,
        out_shape=jax.ShapeDtypeStruct((SEQ, D_SSD), F32),
        grid=(N_CHUNKS,),
        in_specs=common(fwd),
        out_specs=pl.BlockSpec((CHUNK, D_SSD), lambda c: (c, 0)),
        scratch_shapes=[pltpu.VMEM((STATE, D_SSD), F32)],
        compiler_params=_params("arbitrary"),
        name="ssd_fwd",
    )(xbc, xbc, xbc, *decays, e_sel_f)
    return pl.pallas_call(
        _ssd_bwd_kernel,
        out_shape=jax.ShapeDtypeStruct((SEQ, D_SSD), BF16),
        grid=(N_CHUNKS,),
        in_specs=common(bwd) + [
            pl.BlockSpec((CHUNK, D_SSD), lambda c: (bwd(c), 0)),
            pl.BlockSpec((CHUNK, D_SSD), lambda c: (bwd(c), 0)),
            pl.BlockSpec((1, D_SSD), lambda c: (0, 0)),
            pl.BlockSpec((1, D_SSD), lambda c: (0, 0)),
        ],
        out_specs=pl.BlockSpec((CHUNK, D_SSD), lambda c: (bwd(c), 0)),
        scratch_shapes=[pltpu.VMEM((STATE, D_SSD), F32)],
        compiler_params=_params("arbitrary"),
        name="ssd_bwd",
    )(xbc, xbc, xbc, *decays, e_sel_b, y_f, p_zx, d_skip_x, norm_g)


MERGE_BN = 512
MERGE_BM = 2048


def _merge_kernel(ya_ref, sc_ref, wa_ref, wb_ref, ga_ref, gb_ref, o_ref):
    w_a = wa_ref[...].astype(BF16)
    w_b = wb_ref[...].astype(BF16)
    for rows in _row_groups(ya_ref.shape[0]):
        y_a = jnp.dot(ya_ref[rows, :], w_a, preferred_element_type=F32)
        y_b = jnp.dot(sc_ref[rows, :], w_b, preferred_element_type=F32)
        o_ref[rows, :] = (ga_ref[rows, :].astype(F32) * y_a
                          + gb_ref[rows, :].astype(F32) * y_b).astype(o_ref.dtype)


def merge_branches(y_ssd, sc, w_a, w_b, p_sg, layer):
    bn = MERGE_BN
    bm = MERGE_BM
    ga0 = SG_GATE // bn
    gb0 = (SG_GATE + D_MODEL) // bn
    resident = pl.Buffered(1)
    return pl.pallas_call(
        _merge_kernel,
        out_shape=jax.ShapeDtypeStruct((SEQ, D_MODEL), BF16),
        grid=(SEQ // bm, D_MODEL // bn),
        in_specs=[pl.BlockSpec((bm, D_SSD), lambda i, j: (i, 0), pipeline_mode=resident),
                  pl.BlockSpec((bm, D_SC), lambda i, j: (i, 0), pipeline_mode=resident),
                  pl.BlockSpec((None, D_SSD, bn), lambda i, j: (layer, 0, j)),
                  pl.BlockSpec((None, D_SC, bn), lambda i, j: (layer, 0, j)),
                  pl.BlockSpec((bm, bn), lambda i, j: (i, ga0 + j)),
                  pl.BlockSpec((bm, bn), lambda i, j: (i, gb0 + j))],
        out_specs=pl.BlockSpec((bm, bn), lambda i, j: (i, j)),
        compiler_params=_params("parallel", "parallel"),
        name="merge_branches",
    )(y_ssd, sc, w_a, w_b, p_sg, p_sg)


def _emit_residual(o, rows, g_ref, o_ref, xg_ref, ssq_ref):
    o_ref[rows, :] = o
    xg_ref[rows, :] = (o * g_ref[...]).astype(xg_ref.dtype)
    ssq_ref[rows, :] += _sum_squares(o, ssq_ref.at[rows, :])


def _mm_residual_kernel(a_ref, w_ref, x_ref, g_ref, o_ref, xg_ref, ssq_ref):
    @pl.when(pl.program_id(1) == 0)
    def _():
        ssq_ref[...] = jnp.zeros_like(ssq_ref)

    w = w_ref[...].astype(BF16)
    for rows in _row_groups(a_ref.shape[0]):
        o = x_ref[rows, :] + jnp.dot(a_ref[rows, :], w, preferred_element_type=F32)
        _emit_residual(o, rows, g_ref, o_ref, xg_ref, ssq_ref)


def _residual_outs(m, n, bm, bn, index):
    shapes = (jax.ShapeDtypeStruct((m, n), F32), jax.ShapeDtypeStruct((m, n), BF16),
              jax.ShapeDtypeStruct((m, LANES), F32))
    specs = (pl.BlockSpec((bm, bn), index), pl.BlockSpec((bm, bn), index),
             pl.BlockSpec((bm, LANES), lambda i, *_: (i, 0)))
    return shapes, specs


def matmul_residual(a, w, x, g_next, layer, bn=512):
    m, k = a.shape
    n = w.shape[2]
    shapes, specs = _residual_outs(m, n, BM_WIDE, bn, lambda i, j: (i, j))
    return pl.pallas_call(
        _mm_residual_kernel,
        out_shape=shapes,
        grid=(m // BM_WIDE, n // bn),
        in_specs=[pl.BlockSpec((BM_WIDE, k), lambda i, j: (i, 0), pipeline_mode=pl.Buffered(1)),
                  pl.BlockSpec((None, k, bn), lambda i, j: (layer, 0, j)),
                  pl.BlockSpec((BM_WIDE, bn), lambda i, j: (i, j)),
                  pl.BlockSpec((1, bn), lambda i, j: (0, j))],
        out_specs=specs,
        compiler_params=_params("parallel", "arbitrary"),
        name="matmul_residual",
    )(a, w, x, g_next.reshape(1, -1))


FF_BM = 2048


def _ffn_up_kernel(a_ref, ssq_ref, wg_ref, wu_ref, o_ref):
    @pl.when(pl.program_id(1) < FF_BLOCKS)
    def _():
        w_gate = wg_ref[...].astype(BF16)
        w_up = wu_ref[...].astype(BF16)
        for rows in _row_groups(FF_BM):
            a = a_ref[rows, :]
            scale = _row_scale(ssq_ref.at[rows, :], FF_BN)
            gate = jnp.dot(a, w_gate, preferred_element_type=F32) * scale
            up = jnp.dot(a, w_up, preferred_element_type=F32) * scale
            o_ref[rows, :] = (jax.nn.silu(gate) * up).astype(o_ref.dtype)

    @pl.when(pl.program_id(1) >= FF_BLOCKS)
    def _():
        o_ref[...] = jnp.zeros_like(o_ref)


def ffn_up(xg, ssq, w_gate, w_up, layer):
    m, k = xg.shape
    w_spec = pl.BlockSpec((None, k, FF_BN), lambda i, j: (layer, 0, jnp.minimum(j, FF_BLOCKS - 1)))
    return pl.pallas_call(
        _ffn_up_kernel,
        out_shape=jax.ShapeDtypeStruct((m, D_FF_PAD), BF16),
        grid=(m // FF_BM, D_FF_PAD // FF_BN),
        in_specs=[pl.BlockSpec((FF_BM, k), lambda i, j: (i, 0), pipeline_mode=pl.Buffered(1)),
                  pl.BlockSpec((FF_BM, LANES), lambda i, j: (i, 0)),
                  w_spec, w_spec],
        out_specs=pl.BlockSpec((FF_BM, FF_BN), lambda i, j: (i, j)),
        compiler_params=_params("parallel", "parallel"),
        name="ffn_up",
    )(xg, ssq, w_gate, w_up)


def _cast_kernel(w_ref, o_ref):
    o_ref[...] = w_ref[...].astype(o_ref.dtype)


def cast_bf16(w, rows=512):
    depth, k, n = w.shape
    spec = pl.BlockSpec((None, rows, n), lambda l, r: (l, r, 0))
    return pl.pallas_call(
        _cast_kernel,
        out_shape=jax.ShapeDtypeStruct(w.shape, BF16),
        grid=(depth, k // rows),
        in_specs=[spec],
        out_specs=spec,
        compiler_params=_params("parallel", "parallel"),
        name="cast_bf16",
    )(w)


def _cast_pad_kernel(w_ref, o_ref):
    @pl.when(pl.program_id(1) < FF_BLOCKS)
    def _():
        o_ref[...] = w_ref[...].astype(o_ref.dtype)

    @pl.when(pl.program_id(1) >= FF_BLOCKS)
    def _():
        o_ref[...] = jnp.zeros_like(o_ref)


def cast_pad_rows(w_down):
    n = w_down.shape[2]
    return pl.pallas_call(
        _cast_pad_kernel,
        out_shape=jax.ShapeDtypeStruct((DEPTH, D_FF_PAD, n), BF16),
        grid=(DEPTH, D_FF_PAD // FF_BN),
        in_specs=[pl.BlockSpec((None, FF_BN, n), lambda l, r: (l, jnp.minimum(r, FF_BLOCKS - 1), 0))],
        out_specs=pl.BlockSpec((None, FF_BN, n), lambda l, r: (l, r, 0)),
        compiler_params=_params("parallel", "parallel"),
        name="cast_pad_rows",
    )(w_down)


def _ffn_down_kernel(a_ref, w_ref, x_ref, g_ref, o_ref, xg_ref, ssq_ref, acc_ref):
    k = pl.program_id(2)
    last = pl.num_programs(2) - 1
    groups = _row_groups(a_ref.shape[0])
    partial = lambda rows: jnp.dot(a_ref[rows, :], w_ref[...], preferred_element_type=F32)

    @pl.when(k == 0)
    def _():
        for rows in groups:
            acc_ref[rows, :] = x_ref[rows, :] + partial(rows)

    @pl.when(jnp.logical_and(k > 0, k < last))
    def _():
        for rows in groups:
            acc_ref[rows, :] += partial(rows)

    @pl.when(jnp.logical_and(k == last, pl.program_id(1) == 0))
    def _():
        ssq_ref[...] = jnp.zeros_like(ssq_ref)

    @pl.when(k == last)
    def _():
        for rows in groups:
            _emit_residual(acc_ref[rows, :] + partial(rows), rows, g_ref, o_ref, xg_ref, ssq_ref)


def ffn_down(hid, w_down, x, g_next, layer):
    m = hid.shape[0]
    n = w_down.shape[2]
    tk = FFN_DOWN_TK
    shapes, specs = _residual_outs(m, n, BM, BN, lambda i, j, k: (i, j))
    return pl.pallas_call(
        _ffn_down_kernel,
        out_shape=shapes,
        grid=(m // BM, n // BN, D_FF_PAD // tk),
        in_specs=[pl.BlockSpec((BM, tk), lambda i, j, k: (i, k)),
                  pl.BlockSpec((None, tk, BN), lambda i, j, k: (layer, k, j)),
                  pl.BlockSpec((BM, BN), lambda i, j, k: (i, j)),
                  pl.BlockSpec((1, BN), lambda i, j, k: (0, j))],
        out_specs=specs,
        scratch_shapes=[pltpu.VMEM((BM, BN), F32)],
        compiler_params=_params("parallel", "arbitrary", "arbitrary"),
        name="ffn_down",
    )(hid, w_down, x, g_next.reshape(1, -1))


def _head_selection(offset):
    rows = jnp.arange(DT_COLS)[:, None]
    heads = jnp.arange(D_SSD)[None, :] // HEAD_DIM
    return (rows == heads + offset).astype(BF16)


def _pad_lanes(v, width):
    return jnp.pad(v, ((0, 0), (0, width - v.shape[1])))


def kernel(x, mix_norm_g, w_in, b_gate, ssd_conv_w, ssd_conv_b, dt_bias_f, dt_bias_b, a_log_f, a_log_b, d_skip, ssd_norm_g, w_ssd_out, sc_conv_w, w_sc_out, w_o, ffn_norm_g, w_ffn_gate, w_ffn_up, w_ffn_down, final_norm_g):
    w_in_t = jnp.swapaxes(w_in, 1, 2)
    w_down = cast_pad_rows(w_ffn_down)
    w_a, w_b, w_o_bf = cast_bf16(w_ssd_out), cast_bf16(w_sc_out), cast_bf16(w_o)
    dt_bias = _pad_lanes(jnp.concatenate([dt_bias_f, dt_bias_b], axis=1), DT_COLS)
    a_log = _pad_lanes(jnp.concatenate([a_log_f, a_log_b], axis=1), DT_COLS)
    d_skip_x = jnp.repeat(d_skip, HEAD_DIM, axis=1)
    e_sel_f = _head_selection(0)
    e_sel_b = _head_selection(HEADS)

    h = x.reshape(SEQ, D_MODEL)
    xg, ssq = norm_prep(h, mix_norm_g[0])
    for l in range(DEPTH):
        p_zx = inproj(xg, ssq, w_in_t, b_gate[l], l, 0, ZX_COLS, IN_BN, BF16, "inproj_zx")
        dt_raw = inproj(xg, ssq, w_in_t, b_gate[l], l, W_IN_DT, DT_COLS, DT_COLS, F32, "inproj_dt")
        p_sg = inproj(xg, ssq, w_in_t, b_gate[l], l, W_IN_SC, SG_COLS, IN_BN, BF16, "inproj_sg",
                      gate_col0=W_IN_SC + SG_GATE)
        xbc = ssd_conv(p_zx, ssd_conv_w[l], ssd_conv_b[l])
        sc = short_conv(p_sg, sc_conv_w[l])
        y_ssd = ssd_scan(xbc, dt_raw, p_zx, dt_bias[l:l + 1], a_log[l:l + 1], e_sel_f, e_sel_b,
                         d_skip_x[l:l + 1], ssd_norm_g[l:l + 1])
        merged = merge_branches(y_ssd, sc, w_a, w_b, p_sg, l)
        h, xg, ssq = matmul_residual(merged, w_o_bf, h, ffn_norm_g[l], l)
        hid = ffn_up(xg, ssq, w_ffn_gate, w_ffn_up, l)
        g_next = mix_norm_g[l + 1] if l + 1 < DEPTH else final_norm_g
        h, xg, ssq = ffn_down(hid, w_down, h, g_next, l)
    out = rmsnorm(h, final_norm_g, F32)
    return out.reshape(x.shape)
```

```python
import functools

import jax
import jax.numpy as jnp
from jax import lax
from jax.experimental import pallas as pl
from jax.experimental.pallas import tpu as pltpu

F32 = jnp.float32
BF16 = jnp.bfloat16

D_MODEL = 4096
SEQ = 8192
DEPTH = 4
HEAD_DIM = 64
D_SSD = D_MODEL // 2
HEADS = D_SSD // HEAD_DIM
GROUPS = 4
HEADS_PER_GROUP = HEADS // GROUPS
STATE = 128
GROUP_WIDTH = D_SSD // GROUPS
SSD_CONV = 5
CHUNK = 128
N_CHUNKS = SEQ // CHUNK
XBC_DIM = D_SSD + 2 * GROUPS * STATE
D_SC = D_MODEL // 4
SC_CONV = 3
D_FF = 11008
EPS = 1e-6
LOG2_E = 1.4426950408889634

LANES = 128
BF16_SUBLANES = 16
VMEM_LIMIT_BYTES = 56 * 1024 * 1024

W_IN_DT = D_SSD + XBC_DIM
W_IN_SC = W_IN_DT + 2 * HEADS
DT_COLS = LANES
ZX_COLS = D_SSD + XBC_DIM
SG_GATE = 3 * D_SC
SG_COLS = SG_GATE + 2 * D_MODEL

FF_BN = 256
FF_BLOCKS = D_FF // FF_BN
D_FF_PAD = 11264
FFN_DOWN_TK = D_FF_PAD // 4

BM = 1024
BN = 1024
BM_WIDE = 2048
MM_ROWS = 256


def _params(*sem):
    return pltpu.CompilerParams(dimension_semantics=sem, vmem_limit_bytes=VMEM_LIMIT_BYTES)


def _row_scale(ssq_ref, width):
    r = lax.rsqrt(ssq_ref[...] * (1.0 / D_MODEL) + EPS)
    return jnp.concatenate([r] * (width // LANES), axis=1) if width > LANES else r


def _sum_squares(o, like_ref):
    return jnp.broadcast_to(jnp.sum(o * o, axis=-1, keepdims=True), like_ref.shape)


def _norm_prep_kernel(x_ref, g_ref, xg_ref, ssq_ref):
    x = x_ref[...]
    xg_ref[...] = (x * g_ref[...]).astype(xg_ref.dtype)
    ssq_ref[...] = _sum_squares(x, ssq_ref)


def norm_prep(x, g, rows=256):
    m, d = x.shape
    return pl.pallas_call(
        _norm_prep_kernel,
        out_shape=(jax.ShapeDtypeStruct((m, d), BF16), jax.ShapeDtypeStruct((m, LANES), F32)),
        grid=(m // rows,),
        in_specs=[pl.BlockSpec((rows, d), lambda i: (i, 0)),
                  pl.BlockSpec((1, d), lambda i: (0, 0))],
        out_specs=(pl.BlockSpec((rows, d), lambda i: (i, 0)),
                   pl.BlockSpec((rows, LANES), lambda i: (i, 0))),
        compiler_params=_params("parallel"),
        name="norm_prep",
    )(x, g.reshape(1, d))


def _rmsnorm_kernel(x_ref, g_ref, o_ref):
    x = x_ref[...]
    y = x * lax.rsqrt(jnp.mean(x * x, axis=-1, keepdims=True) + EPS)
    o_ref[...] = (y * g_ref[...]).astype(o_ref.dtype)


def rmsnorm(x, g, out_dtype, rows=256):
    m, d = x.shape
    return pl.pallas_call(
        _rmsnorm_kernel,
        out_shape=jax.ShapeDtypeStruct((m, d), out_dtype),
        grid=(m // rows,),
        in_specs=[pl.BlockSpec((rows, d), lambda i: (i, 0)),
                  pl.BlockSpec((1, d), lambda i: (0, 0))],
        out_specs=pl.BlockSpec((rows, d), lambda i: (i, 0)),
        compiler_params=_params("parallel"),
        name="rmsnorm",
    )(x, g.reshape(1, d))


def _row_groups(n_rows):
    return [slice(r, r + MM_ROWS) for r in range(0, n_rows, MM_ROWS)]


IN_BN = 512
_NT = (((1,), (1,)), ((), ()))


def _inproj_kernel(a_ref, ssq_ref, wt_ref, b_ref, *rest, gate_block0):
    if len(rest) == 3:
        wdt_ref, o_ref, dt_ref = rest

        @pl.when(pl.program_id(1) == 0)
        def _():
            w_dt = wdt_ref[0].astype(BF16)
            for rows in _row_groups(a_ref.shape[0]):
                acc = lax.dot_general(a_ref[rows, :], w_dt, _NT, preferred_element_type=F32)
                dt_ref[rows, :] = acc * _row_scale(ssq_ref.at[rows, :], DT_COLS)
    else:
        (o_ref,) = rest

    w_t = wt_ref[0].astype(BF16)
    for rows in _row_groups(a_ref.shape[0]):
        acc = lax.dot_general(a_ref[rows, :], w_t, _NT, preferred_element_type=F32)
        v = acc * _row_scale(ssq_ref.at[rows, :], o_ref.shape[1])
        if gate_block0 is not None:
            v = jnp.where(pl.program_id(1) >= gate_block0, jax.nn.sigmoid(v + b_ref[...]), v)
        o_ref[rows, :] = v.astype(o_ref.dtype)


def inproj(xg, ssq, w_in_t, b_gate, layer, col0, n_cols, bn, name, gate_col0=None, dt_col0=None):
    m, k = xg.shape
    gate_block0 = None if gate_col0 is None else (gate_col0 - col0) // bn
    gate_index = lambda j: 0 if gate_block0 is None else jnp.maximum(j - gate_block0, 0)
    w_rows = lambda n, index: pl.BlockSpec((pl.Element(1), pl.Element(n), pl.Element(k)), index)
    in_specs = [pl.BlockSpec((BM_WIDE, k), lambda i, j: (i, 0), pipeline_mode=pl.Buffered(1)),
                pl.BlockSpec((BM_WIDE, LANES), lambda i, j: (i, 0)),
                w_rows(bn, lambda i, j: (layer, pl.multiple_of(col0 + j * bn, 64), 0)),
                pl.BlockSpec((1, bn), lambda i, j: (0, gate_index(j)))]
    out_shape = jax.ShapeDtypeStruct((m, n_cols), BF16)
    out_specs = pl.BlockSpec((BM_WIDE, bn), lambda i, j: (i, j))
    operands = [xg, ssq, w_in_t, b_gate.reshape(1, -1)]
    if dt_col0 is not None:
        in_specs.append(w_rows(DT_COLS, lambda i, j: (layer, dt_col0, 0)))
        operands.append(w_in_t)
        out_shape = (out_shape, jax.ShapeDtypeStruct((m, DT_COLS), F32))
        out_specs = (out_specs, pl.BlockSpec((BM_WIDE, DT_COLS), lambda i, j: (i, 0)))
    return pl.pallas_call(
        functools.partial(_inproj_kernel, gate_block0=gate_block0),
        out_shape=out_shape,
        grid=(m // BM_WIDE, n_cols // bn),
        in_specs=in_specs,
        out_specs=out_specs,
        compiler_params=_params("parallel", "arbitrary"),
        name=name,
    )(*operands)


CONV_ROWS = 1024
CONV_COLS = 256
HALO = BF16_SUBLANES


def _with_halo(prev_ref, cur_ref, next_ref):
    i = pl.program_id(1)
    last = pl.num_programs(1) - 1
    prev = jnp.where(i == 0, 0.0, prev_ref[...].astype(F32))
    nxt = jnp.where(i == last, 0.0, next_ref[...].astype(F32))
    return jnp.concatenate([prev, cur_ref[...].astype(F32), nxt], axis=0)


def _centred_conv(u, w, width):
    pad = width // 2
    out = u[HALO - pad:HALO - pad + CONV_ROWS, :] * w[0:1, :]
    for t in range(1, width):
        out = out + u[HALO - pad + t:HALO - pad + t + CONV_ROWS, :] * w[t:t + 1, :]
    return out


def _halo_specs(col_block0):
    r = CONV_ROWS // HALO
    n_halo_blocks = SEQ // HALO
    return [
        pl.BlockSpec((HALO, CONV_COLS), lambda j, i: (jnp.maximum(i * r - 1, 0), col_block0 + j)),
        pl.BlockSpec((CONV_ROWS, CONV_COLS), lambda j, i: (i, col_block0 + j)),
        pl.BlockSpec((HALO, CONV_COLS),
                     lambda j, i: (jnp.minimum((i + 1) * r, n_halo_blocks - 1), col_block0 + j)),
    ]


def _ssd_conv_kernel(prev_ref, cur_ref, next_ref, w_ref, b_ref, o_ref):
    u = _with_halo(prev_ref, cur_ref, next_ref)
    o_ref[...] = jax.nn.silu(_centred_conv(u, w_ref[...], SSD_CONV) + b_ref[...]).astype(o_ref.dtype)


def ssd_conv(p_zx, conv_w, conv_b):
    return pl.pallas_call(
        _ssd_conv_kernel,
        out_shape=jax.ShapeDtypeStruct((SEQ, XBC_DIM), BF16),
        grid=(XBC_DIM // CONV_COLS, SEQ // CONV_ROWS),
        in_specs=_halo_specs(D_SSD // CONV_COLS) + [
            pl.BlockSpec((SSD_CONV, CONV_COLS), lambda j, i: (0, j)),
            pl.BlockSpec((1, CONV_COLS), lambda j, i: (0, j))],
        out_specs=pl.BlockSpec((CONV_ROWS, CONV_COLS), lambda j, i: (i, j)),
        compiler_params=_params("parallel", "parallel"),
        name="ssd_conv",
    )(p_zx, p_zx, p_zx, conv_w, conv_b.reshape(1, -1))


def _short_conv_kernel(b_ref, cprev_ref, c_ref, cnext_ref, hprev_ref, h_ref, hnext_ref, w_ref, o_ref):
    u = _with_halo(cprev_ref, c_ref, cnext_ref) * _with_halo(hprev_ref, h_ref, hnext_ref)
    o_ref[...] = (b_ref[...].astype(F32) * _centred_conv(u, w_ref[...], SC_CONV)).astype(o_ref.dtype)


def short_conv(p_sg, conv_w):
    c0 = D_SC // CONV_COLS
    h0 = 2 * D_SC // CONV_COLS
    return pl.pallas_call(
        _short_conv_kernel,
        out_shape=jax.ShapeDtypeStruct((SEQ, D_SC), BF16),
        grid=(D_SC // CONV_COLS, SEQ // CONV_ROWS),
        in_specs=[pl.BlockSpec((CONV_ROWS, CONV_COLS), lambda j, i: (i, j))]
        + _halo_specs(c0) + _halo_specs(h0)
        + [pl.BlockSpec((SC_CONV, CONV_COLS), lambda j, i: (0, j))],
        out_specs=pl.BlockSpec((CONV_ROWS, CONV_COLS), lambda j, i: (i, j)),
        compiler_params=_params("parallel", "parallel"),
        name="short_conv",
    )(p_sg, p_sg, p_sg, p_sg, p_sg, p_sg, p_sg, conv_w)


def _expand_heads(v, e_hi):
    hi = v.astype(BF16)
    lo = (v - hi.astype(F32)).astype(BF16)
    return (jnp.dot(hi, e_hi, preferred_element_type=F32)
            + jnp.dot(lo, e_hi, preferred_element_type=F32))


PREP_CHUNKS = 8


def _ssd_prep_kernel(dt_ref, dtb_ref, alog_ref, g2_ref, srct_ref, w_ref, dout_ref):
    row = lax.broadcasted_iota(jnp.int32, (CHUNK, CHUNK), 0)
    col = lax.broadcasted_iota(jnp.int32, (CHUNK, CHUNK), 1)
    lower = (row >= col).astype(F32)
    upper = (row <= col).astype(F32)
    is_fwd = lax.broadcasted_iota(jnp.int32, (CHUNK, DT_COLS), 1) < HEADS
    a_neg = -jnp.exp(alog_ref[...])
    for c in range(PREP_CHUNKS):
        rows = slice(c * CHUNK, (c + 1) * CHUNK)
        dt = jax.nn.softplus(dt_ref[rows, :] + dtb_ref[...])
        a = dt * a_neg
        hp = lax.Precision.HIGHEST
        g = (jnp.dot(lower, jnp.where(is_fwd, a, 0.0), preferred_element_type=F32, precision=hp)
             + jnp.dot(upper, jnp.where(is_fwd, 0.0, a), preferred_element_type=F32, precision=hp))
        g2 = LOG2_E * g
        g2_tot = jnp.where(is_fwd[0:1, :], g2[CHUNK - 1:CHUNK, :], g2[0:1, :])
        g2_ref[rows, :] = g2
        srct_ref[rows, :] = (g2 - jnp.log2(dt)).T
        w_ref[rows, :] = dt * jnp.exp2(g2_tot - g2)
        dout_ref[c * 8:(c + 1) * 8, :] = jnp.broadcast_to(jnp.exp2(g2_tot), (8, DT_COLS))


def ssd_prep(dt_raw, dt_bias, a_log):
    rows = PREP_CHUNKS * CHUNK
    full = jax.ShapeDtypeStruct((SEQ, DT_COLS), F32)
    blk = pl.BlockSpec((rows, DT_COLS), lambda i: (i, 0))
    vec = pl.BlockSpec((1, DT_COLS), lambda i: (0, 0))
    return pl.pallas_call(
        _ssd_prep_kernel,
        out_shape=(full, full, full, jax.ShapeDtypeStruct((N_CHUNKS * 8, DT_COLS), F32)),
        grid=(SEQ // rows,),
        in_specs=[blk, vec, vec],
        out_specs=(blk, blk, blk, pl.BlockSpec((PREP_CHUNKS * 8, DT_COLS), lambda i: (i, 0))),
        compiler_params=_params("parallel"),
        name="ssd_prep",
    )(dt_raw, dt_bias, a_log)


def _ssd_chunk(xs_ref, b_ref, c_ref, g2_ref, srct_ref, w_ref, dout_ref, e_ref, state_ref, *, reverse):
    off = HEADS if reverse else 0
    row = lax.broadcasted_iota(jnp.int32, (CHUNK, CHUNK), 0)
    col = lax.broadcasted_iota(jnp.int32, (CHUNK, CHUNK), 1)
    mask = (row <= col) if reverse else (row >= col)
    g2 = g2_ref[...]
    src_t = srct_ref[...]

    xs = xs_ref[...]
    state = state_ref[...]
    state_bf = state.astype(BF16)
    lane_in_tile = lax.broadcasted_iota(jnp.int32, (CHUNK + STATE, LANES), 1)

    ys = []
    for grp in range(GROUPS):
        b_g = b_ref[:, grp * STATE:(grp + 1) * STATE]
        c_g = c_ref[:, grp * STATE:(grp + 1) * STATE]
        cb = lax.dot_general(c_g, b_g, (((1,), (1,)), ((), ())), preferred_element_type=F32)
        c_f = c_g.astype(F32)
        for r in range(HEADS_PER_GROUP):
            h = grp * HEADS_PER_GROUP + r
            g_col = jnp.broadcast_to(g2[:, off + h:off + h + 1], (CHUNK, CHUNK))
            lmat_dt = jnp.where(mask, jnp.exp2(g_col - src_t[off + h:off + h + 1, :]), 0.0)
            m_h = (cb * lmat_dt).astype(BF16)
            c_s = (c_f * jnp.exp2(g_col)).astype(BF16)
            lhs = jnp.concatenate([m_h, c_s], axis=1)
            pair = slice((h // 2) * LANES, (h // 2 + 1) * LANES)
            own = (lane_in_tile >= HEAD_DIM) if h % 2 else (lane_in_tile < HEAD_DIM)
            rhs = jnp.concatenate([xs[:, pair], state_bf[:, pair]], axis=0)
            part = jnp.dot(lhs, jnp.where(own, rhs, jnp.zeros_like(rhs)), preferred_element_type=F32)
            if h % 2:
                ys.append(ys.pop() + part)
            else:
                ys.append(part)
    y = jnp.concatenate(ys, axis=1)

    e_hi = e_ref[...]
    xw = (xs.astype(F32) * _expand_heads(w_ref[...], e_hi)).astype(BF16)
    decay = _expand_heads(dout_ref[...], e_hi)[0:1, :]
    new = []
    for grp in range(GROUPS):
        b_g = b_ref[:, grp * STATE:(grp + 1) * STATE]
        new.append(lax.dot_general(b_g, xw[:, grp * GROUP_WIDTH:(grp + 1) * GROUP_WIDTH],
                                   (((0,), (0,)), ((), ())), preferred_element_type=F32))
    state_ref[...] = state * decay + jnp.concatenate(new, axis=1)
    return y


def _ssd_fwd_kernel(*refs):
    *chunk_refs, y_ref, state_ref = refs

    @pl.when(pl.program_id(0) == 0)
    def _():
        state_ref[...] = jnp.zeros_like(state_ref)

    y_ref[...] = _ssd_chunk(*chunk_refs, state_ref, reverse=False)


def _ssd_bwd_kernel(*refs):
    *chunk_refs, yf_ref, z_ref, dskip_ref, ng_ref, o_ref, state_ref = refs
    xs_ref = chunk_refs[0]

    @pl.when(pl.program_id(0) == 0)
    def _():
        state_ref[...] = jnp.zeros_like(state_ref)

    y = _ssd_chunk(*chunk_refs, state_ref, reverse=True)
    y = y + yf_ref[...] + dskip_ref[...] * xs_ref[...].astype(F32)
    y = y * jax.nn.silu(z_ref[...].astype(F32))
    ng = ng_ref[...]
    outs = []
    for grp in range(GROUPS):
        sl = slice(grp * GROUP_WIDTH, (grp + 1) * GROUP_WIDTH)
        yg = y[:, sl]
        yg = yg * lax.rsqrt(jnp.mean(yg * yg, axis=-1, keepdims=True) + EPS)
        outs.append(yg * ng[:, sl])
    o_ref[...] = jnp.concatenate(outs, axis=1).astype(o_ref.dtype)


def ssd_scan(xbc, dt_raw, p_zx, dt_bias, a_log, e_sel_f, e_sel_b, d_skip_x, norm_g):
    gn = GROUPS * STATE
    b_blk = D_SSD // gn
    decays = ssd_prep(dt_raw, dt_bias, a_log)
    per_chunk = lambda cm: pl.BlockSpec((CHUNK, DT_COLS), lambda c: (cm(c), 0))
    common = lambda cm: [
        pl.BlockSpec((CHUNK, D_SSD), lambda c: (cm(c), 0)),
        pl.BlockSpec((CHUNK, gn), lambda c: (cm(c), b_blk)),
        pl.BlockSpec((CHUNK, gn), lambda c: (cm(c), b_blk + 1)),
        per_chunk(cm), per_chunk(cm), per_chunk(cm),
        pl.BlockSpec((8, DT_COLS), lambda c: (cm(c), 0)),
        pl.BlockSpec((DT_COLS, D_SSD), lambda c: (0, 0)),
    ]
    fwd = lambda c: c
    bwd = lambda c: N_CHUNKS - 1 - c
    y_f = pl.pallas_call(
        _ssd_fwd_kernel,
        out_shape=jax.ShapeDtypeStruct((SEQ, D_SSD), F32),
        grid=(N_CHUNKS,),
        in_specs=common(fwd),
        out_specs=pl.BlockSpec((CHUNK, D_SSD), lambda c: (c, 0)),
        scratch_shapes=[pltpu.VMEM((STATE, D_SSD), F32)],
        compiler_params=_params("arbitrary"),
        name="ssd_fwd",
    )(xbc, xbc, xbc, *decays, e_sel_f)
    return pl.pallas_call(
        _ssd_bwd_kernel,
        out_shape=jax.ShapeDtypeStruct((SEQ, D_SSD), BF16),
        grid=(N_CHUNKS,),
        in_specs=common(bwd) + [
            pl.BlockSpec((CHUNK, D_SSD), lambda c: (bwd(c), 0)),
            pl.BlockSpec((CHUNK, D_SSD), lambda c: (bwd(c), 0)),
            pl.BlockSpec((1, D_SSD), lambda c: (0, 0)),
            pl.BlockSpec((1, D_SSD), lambda c: (0, 0)),
        ],
        out_specs=pl.BlockSpec((CHUNK, D_SSD), lambda c: (bwd(c), 0)),
        scratch_shapes=[pltpu.VMEM((STATE, D_SSD), F32)],
        compiler_params=_params("arbitrary"),
        name="ssd_bwd",
    )(xbc, xbc, xbc, *decays, e_sel_b, y_f, p_zx, d_skip_x, norm_g)


MERGE_BN = 512
MERGE_BM = 2048


def _merge_kernel(ya_ref, sc_ref, wa_ref, wb_ref, ga_ref, gb_ref, o_ref):
    w_a = wa_ref[...].astype(BF16)
    w_b = wb_ref[...].astype(BF16)
    for rows in _row_groups(ya_ref.shape[0]):
        y_a = jnp.dot(ya_ref[rows, :], w_a, preferred_element_type=F32)
        y_b = jnp.dot(sc_ref[rows, :], w_b, preferred_element_type=F32)
        o_ref[rows, :] = (ga_ref[rows, :].astype(F32) * y_a
                          + gb_ref[rows, :].astype(F32) * y_b).astype(o_ref.dtype)


def merge_branches(y_ssd, sc, w_a, w_b, p_sg, layer):
    bn = MERGE_BN
    bm = MERGE_BM
    ga0 = SG_GATE // bn
    gb0 = (SG_GATE + D_MODEL) // bn
    resident = pl.Buffered(1)
    return pl.pallas_call(
        _merge_kernel,
        out_shape=jax.ShapeDtypeStruct((SEQ, D_MODEL), BF16),
        grid=(SEQ // bm, D_MODEL // bn),
        in_specs=[pl.BlockSpec((bm, D_SSD), lambda i, j: (i, 0), pipeline_mode=resident),
                  pl.BlockSpec((bm, D_SC), lambda i, j: (i, 0), pipeline_mode=resident),
                  pl.BlockSpec((None, D_SSD, bn), lambda i, j: (layer, 0, j)),
                  pl.BlockSpec((None, D_SC, bn), lambda i, j: (layer, 0, j)),
                  pl.BlockSpec((bm, bn), lambda i, j: (i, ga0 + j)),
                  pl.BlockSpec((bm, bn), lambda i, j: (i, gb0 + j))],
        out_specs=pl.BlockSpec((bm, bn), lambda i, j: (i, j)),
        compiler_params=_params("parallel", "parallel"),
        name="merge_branches",
    )(y_ssd, sc, w_a, w_b, p_sg, p_sg)


def _emit_residual(o, rows, g_ref, o_ref, xg_ref, ssq_ref):
    o_ref[rows, :] = o
    xg_ref[rows, :] = (o * g_ref[...]).astype(xg_ref.dtype)
    ssq_ref[rows, :] += _sum_squares(o, ssq_ref.at[rows, :])


def _mm_residual_kernel(a_ref, w_ref, x_ref, g_ref, o_ref, xg_ref, ssq_ref):
    @pl.when(pl.program_id(1) == 0)
    def _():
        ssq_ref[...] = jnp.zeros_like(ssq_ref)

    w = w_ref[...].astype(BF16)
    for rows in _row_groups(a_ref.shape[0]):
        o = x_ref[rows, :] + jnp.dot(a_ref[rows, :], w, preferred_element_type=F32)
        _emit_residual(o, rows, g_ref, o_ref, xg_ref, ssq_ref)


def _residual_outs(m, n, bm, bn, index):
    shapes = (jax.ShapeDtypeStruct((m, n), F32), jax.ShapeDtypeStruct((m, n), BF16),
              jax.ShapeDtypeStruct((m, LANES), F32))
    specs = (pl.BlockSpec((bm, bn), index), pl.BlockSpec((bm, bn), index),
             pl.BlockSpec((bm, LANES), lambda i, *_: (i, 0)))
    return shapes, specs


def matmul_residual(a, w, x, g_next, layer, bn=512):
    m, k = a.shape
    n = w.shape[2]
    shapes, specs = _residual_outs(m, n, BM_WIDE, bn, lambda i, j: (i, j))
    return pl.pallas_call(
        _mm_residual_kernel,
        out_shape=shapes,
        grid=(m // BM_WIDE, n // bn),
        in_specs=[pl.BlockSpec((BM_WIDE, k), lambda i, j: (i, 0), pipeline_mode=pl.Buffered(1)),
                  pl.BlockSpec((None, k, bn), lambda i, j: (layer, 0, j)),
                  pl.BlockSpec((BM_WIDE, bn), lambda i, j: (i, j)),
                  pl.BlockSpec((1, bn), lambda i, j: (0, j))],
        out_specs=specs,
        compiler_params=_params("parallel", "arbitrary"),
        name="matmul_residual",
    )(a, w, x, g_next.reshape(1, -1))


FF_BM = 2048


def _ffn_up_kernel(a_ref, ssq_ref, wg_ref, wu_ref, o_ref):
    @pl.when(pl.program_id(1) < FF_BLOCKS)
    def _():
        w_gate = wg_ref[...].astype(BF16)
        w_up = wu_ref[...].astype(BF16)
        for rows in _row_groups(FF_BM):
            a = a_ref[rows, :]
            scale = _row_scale(ssq_ref.at[rows, :], FF_BN)
            gate = jnp.dot(a, w_gate, preferred_element_type=F32) * scale
            up = jnp.dot(a, w_up, preferred_element_type=F32) * scale
            o_ref[rows, :] = (jax.nn.silu(gate) * up).astype(o_ref.dtype)

    @pl.when(pl.program_id(1) >= FF_BLOCKS)
    def _():
        o_ref[...] = jnp.zeros_like(o_ref)


def ffn_up(xg, ssq, w_gate, w_up, layer):
    m, k = xg.shape
    w_spec = pl.BlockSpec((None, k, FF_BN), lambda i, j: (layer, 0, jnp.minimum(j, FF_BLOCKS - 1)))
    return pl.pallas_call(
        _ffn_up_kernel,
        out_shape=jax.ShapeDtypeStruct((m, D_FF_PAD), BF16),
        grid=(m // FF_BM, D_FF_PAD // FF_BN),
        in_specs=[pl.BlockSpec((FF_BM, k), lambda i, j: (i, 0), pipeline_mode=pl.Buffered(1)),
                  pl.BlockSpec((FF_BM, LANES), lambda i, j: (i, 0)),
                  w_spec, w_spec],
        out_specs=pl.BlockSpec((FF_BM, FF_BN), lambda i, j: (i, j)),
        compiler_params=_params("parallel", "parallel"),
        name="ffn_up",
    )(xg, ssq, w_gate, w_up)


def _cast_kernel(w_ref, o_ref):
    o_ref[...] = w_ref[...].astype(o_ref.dtype)


def cast_bf16(w, rows=512):
    depth, k, n = w.shape
    spec = pl.BlockSpec((None, rows, n), lambda l, r: (l, r, 0))
    return pl.pallas_call(
        _cast_kernel,
        out_shape=jax.ShapeDtypeStruct(w.shape, BF16),
        grid=(depth, k // rows),
        in_specs=[spec],
        out_specs=spec,
        compiler_params=_params("parallel", "parallel"),
        name="cast_bf16",
    )(w)


def _cast_pad_kernel(w_ref, o_ref):
    @pl.when(pl.program_id(1) < FF_BLOCKS)
    def _():
        o_ref[...] = w_ref[...].astype(o_ref.dtype)

    @pl.when(pl.program_id(1) >= FF_BLOCKS)
    def _():
        o_ref[...] = jnp.zeros_like(o_ref)


def cast_pad_rows(w_down):
    n = w_down.shape[2]
    return pl.pallas_call(
        _cast_pad_kernel,
        out_shape=jax.ShapeDtypeStruct((DEPTH, D_FF_PAD, n), BF16),
        grid=(DEPTH, D_FF_PAD // FF_BN),
        in_specs=[pl.BlockSpec((None, FF_BN, n), lambda l, r: (l, jnp.minimum(r, FF_BLOCKS - 1), 0))],
        out_specs=pl.BlockSpec((None, FF_BN, n), lambda l, r: (l, r, 0)),
        compiler_params=_params("parallel", "parallel"),
        name="cast_pad_rows",
    )(w_down)


def _ffn_down_kernel(a_ref, w_ref, x_ref, g_ref, o_ref, xg_ref, ssq_ref, acc_ref):
    k = pl.program_id(2)
    last = pl.num_programs(2) - 1
    groups = _row_groups(a_ref.shape[0])
    partial = lambda rows: jnp.dot(a_ref[rows, :], w_ref[...], preferred_element_type=F32)

    @pl.when(k == 0)
    def _():
        for rows in groups:
            acc_ref[rows, :] = x_ref[rows, :] + partial(rows)

    @pl.when(jnp.logical_and(k > 0, k < last))
    def _():
        for rows in groups:
            acc_ref[rows, :] += partial(rows)

    @pl.when(jnp.logical_and(k == last, pl.program_id(1) == 0))
    def _():
        ssq_ref[...] = jnp.zeros_like(ssq_ref)

    @pl.when(k == last)
    def _():
        for rows in groups:
            _emit_residual(acc_ref[rows, :] + partial(rows), rows, g_ref, o_ref, xg_ref, ssq_ref)


def ffn_down(hid, w_down, x, g_next, layer):
    m = hid.shape[0]
    n = w_down.shape[2]
    tk = FFN_DOWN_TK
    shapes, specs = _residual_outs(m, n, BM, BN, lambda i, j, k: (i, j))
    return pl.pallas_call(
        _ffn_down_kernel,
        out_shape=shapes,
        grid=(m // BM, n // BN, D_FF_PAD // tk),
        in_specs=[pl.BlockSpec((BM, tk), lambda i, j, k: (i, k)),
                  pl.BlockSpec((None, tk, BN), lambda i, j, k: (layer, k, j)),
                  pl.BlockSpec((BM, BN), lambda i, j, k: (i, j)),
                  pl.BlockSpec((1, BN), lambda i, j, k: (0, j))],
        out_specs=specs,
        scratch_shapes=[pltpu.VMEM((BM, BN), F32)],
        compiler_params=_params("parallel", "arbitrary", "arbitrary"),
        name="ffn_down",
    )(hid, w_down, x, g_next.reshape(1, -1))


def _head_selection(offset):
    rows = jnp.arange(DT_COLS)[:, None]
    heads = jnp.arange(D_SSD)[None, :] // HEAD_DIM
    return (rows == heads + offset).astype(BF16)


def _pad_lanes(v, width):
    return jnp.pad(v, ((0, 0), (0, width - v.shape[1])))


def kernel(x, mix_norm_g, w_in, b_gate, ssd_conv_w, ssd_conv_b, dt_bias_f, dt_bias_b, a_log_f, a_log_b, d_skip, ssd_norm_g, w_ssd_out, sc_conv_w, w_sc_out, w_o, ffn_norm_g, w_ffn_gate, w_ffn_up, w_ffn_down, final_norm_g):
    w_in_t = jnp.swapaxes(w_in, 1, 2)
    w_down = cast_pad_rows(w_ffn_down)
    w_o_bf = cast_bf16(w_o)
    dt_bias = _pad_lanes(jnp.concatenate([dt_bias_f, dt_bias_b], axis=1), DT_COLS)
    a_log = _pad_lanes(jnp.concatenate([a_log_f, a_log_b], axis=1), DT_COLS)
    d_skip_x = jnp.repeat(d_skip, HEAD_DIM, axis=1)
    e_sel_f = _head_selection(0)
    e_sel_b = _head_selection(HEADS)

    h = x.reshape(SEQ, D_MODEL)
    xg, ssq = norm_prep(h, mix_norm_g[0])
    for l in range(DEPTH):
        p_zx, dt_raw = inproj(xg, ssq, w_in_t, b_gate[l], l, 0, ZX_COLS, IN_BN, "inproj_zx", dt_col0=W_IN_DT)
        p_sg = inproj(xg, ssq, w_in_t, b_gate[l], l, W_IN_SC, SG_COLS, IN_BN, "inproj_sg",
                      gate_col0=W_IN_SC + SG_GATE)
        xbc = ssd_conv(p_zx, ssd_conv_w[l], ssd_conv_b[l])
        sc = short_conv(p_sg, sc_conv_w[l])
        y_ssd = ssd_scan(xbc, dt_raw, p_zx, dt_bias[l:l + 1], a_log[l:l + 1], e_sel_f, e_sel_b,
                         d_skip_x[l:l + 1], ssd_norm_g[l:l + 1])
        merged = merge_branches(y_ssd, sc, w_ssd_out, w_sc_out, p_sg, l)
        h, xg, ssq = matmul_residual(merged, w_o_bf, h, ffn_norm_g[l], l)
        hid = ffn_up(xg, ssq, w_ffn_gate, w_ffn_up, l)
        g_next = mix_norm_g[l + 1] if l + 1 < DEPTH else final_norm_g
        h, xg, ssq = ffn_down(hid, w_down, h, g_next, l)
    out = rmsnorm(h, final_norm_g, F32)
    return out.reshape(x.shape)
```
